```python
import math
import jax, jax.numpy as jnp
from jax import lax
import numpy as np

D_MODEL = 1024
BATCH = 8
SEQ = 4096
DEPTH = 1
DEC_BATCH = 4
DEC_SEQ = 4096
PAST_LEN = 128

ATT_HEADS = 16
ATT_KV_HEADS = 4
ATT_REP = ATT_HEADS // ATT_KV_HEADS
HEAD_DIM = 64
ATT_WIDTH = ATT_HEADS * HEAD_DIM
KV_WIDTH = ATT_KV_HEADS * HEAD_DIM
WINDOW = 128
BLOCK = 128
SSM_INNER = 2 * D_MODEL
SSM_HEAD_DIM = 64
SSM_HEADS = SSM_INNER // SSM_HEAD_DIM
SSM_GROUPS = 4
SSM_REP = SSM_HEADS // SSM_GROUPS
SSM_STATE = 128
SSM_GN = SSM_GROUPS * SSM_STATE
CONV_WIDTH = 5
CONV_DIM = SSM_INNER + 2 * SSM_GN
CHUNK = 128
N_BRANCH = 2
N_IN = ATT_WIDTH + 2 * KV_WIDTH + SSM_INNER + CONV_DIM + 2 * SSM_HEADS + N_BRANCH * D_MODEL
MEM_TOKENS = 256
X_HEADS = 4
X_HEAD_DIM = D_MODEL // X_HEADS
X_WIDTH = X_HEADS * X_HEAD_DIM
FFN_HIDDEN = -(-8 * D_MODEL // (3 * 256)) * 256
EPS = 1e-6

kernel_name = "hybrid_bidir_swa_ssd_encoder"


def rms_norm(x, g):
    x32 = x.astype(jnp.float32)
    y = x32 * lax.rsqrt(jnp.mean(x32 * x32, axis=-1, keepdims=True) + EPS)
    return (y * g.astype(jnp.float32)).astype(x.dtype)


def alibi_slopes(n_heads):
    return jnp.exp2(-8.0 * jnp.arange(1, n_heads + 1, dtype=jnp.float32) / n_heads)


def windowed_gqa(q, k, v, sink):
    b, L, _ = q.shape
    nb = L // BLOCK
    qb = q.reshape(b, nb, BLOCK, ATT_KV_HEADS, ATT_REP, HEAD_DIM)
    pad = ((0, 0), (BLOCK, BLOCK), (0, 0))
    kp = jnp.pad(k, pad).reshape(b, nb + 2, BLOCK, ATT_KV_HEADS, HEAD_DIM)
    vp = jnp.pad(v, pad).reshape(b, nb + 2, BLOCK, ATT_KV_HEADS, HEAD_DIM)
    kb = jnp.concatenate([kp[:, :-2], kp[:, 1:-1], kp[:, 2:]], axis=2)
    vb = jnp.concatenate([vp[:, :-2], vp[:, 1:-1], vp[:, 2:]], axis=2)
    s = jnp.einsum('bnqgrd,bnkgd->bngrqk', qb, kb).astype(jnp.float32) * (HEAD_DIM ** -0.5)
    blk = jnp.arange(nb)[:, None] * BLOCK
    qpos = blk + jnp.arange(BLOCK)[None, :]
    kpos = blk - BLOCK + jnp.arange(3 * BLOCK)[None, :]
    dist = jnp.abs(qpos[:, :, None] - kpos[:, None, :])
    valid = (dist <= WINDOW) & ((kpos >= 0) & (kpos < L))[:, None, :]
    slopes = alibi_slopes(ATT_HEADS).reshape(ATT_KV_HEADS, ATT_REP)
    bias = -slopes[None, :, :, None, None] * dist[:, None, None].astype(jnp.float32)
    s = jnp.where(valid[:, None, None], s + bias, -jnp.inf)
    sink_b = sink.astype(jnp.float32).reshape(ATT_KV_HEADS, ATT_REP)[None, None, :, :, None, None]
    m = jnp.maximum(jnp.max(s, axis=-1, keepdims=True), sink_b)
    p = jnp.exp(s - m)
    denom = jnp.sum(p, axis=-1, keepdims=True) + jnp.exp(sink_b - m)
    probs = (p / denom).astype(v.dtype)
    o = jnp.einsum('bngrqk,bnkgd->bnqgrd', probs, vb)
    return o.reshape(b, L, ATT_WIDTH)


def ssd_chunked(x, dt, a, bm, cm):
    b, L = x.shape[:2]
    nc = L // CHUNK
    x = x.reshape(b, nc, CHUNK, SSM_GROUPS, SSM_REP, SSM_HEAD_DIM)
    dt = dt.reshape(b, nc, CHUNK, SSM_GROUPS, SSM_REP)
    bm = bm.reshape(b, nc, CHUNK, SSM_GROUPS, SSM_STATE)
    cm = cm.reshape(b, nc, CHUNK, SSM_GROUPS, SSM_STATE)
    acum = jnp.cumsum(jnp.moveaxis(dt * a, 2, -1), axis=-1)
    xdt = x * dt[..., None]
    lower = jnp.tril(jnp.ones((CHUNK, CHUNK), dtype=bool))
    seg = acum[..., :, None] - acum[..., None, :]
    decay = jnp.exp(jnp.where(lower, seg, -jnp.inf))
    cb = jnp.einsum('bclgn,bcsgn->bcgls', cm, bm)
    y_diag = jnp.einsum('bcgrls,bcsgrp->bclgrp', cb[:, :, :, None] * decay, xdt)
    decay_states = jnp.exp(acum[..., -1:] - acum)
    states = jnp.einsum('bcsgn,bcgrs,bcsgrp->bcgrpn', bm, decay_states, xdt)
    chunk_decay = jnp.exp(acum[..., -1])

    def step(carry, inp):
        st, dec = inp
        return carry * dec[..., None, None] + st, carry

    init = jnp.zeros((b, SSM_GROUPS, SSM_REP, SSM_HEAD_DIM, SSM_STATE), jnp.float32)
    _, s_in = lax.scan(step, init, (jnp.moveaxis(states, 1, 0), jnp.moveaxis(chunk_decay, 1, 0)))
    s_in = jnp.moveaxis(s_in, 0, 1)
    y_off = jnp.einsum('bclgn,bcgrpn,bcgrl->bclgrp', cm, s_in, jnp.exp(acum))
    return (y_diag + y_off).reshape(b, L, SSM_GROUPS, SSM_REP, SSM_HEAD_DIM)


def ssd_branch(xbc, z, dt_raw, conv_w, conv_b, a_log, dt_bias, d_skip, ssm_norm):
    b, L, _ = xbc.shape
    xbc = lax.conv_general_dilated(
        xbc, conv_w[:, None, :].astype(xbc.dtype), (1,),
        [(CONV_WIDTH // 2, CONV_WIDTH // 2)],
        dimension_numbers=('NWC', 'WIO', 'NWC'), feature_group_count=CONV_DIM)
    xbc = jax.nn.silu(xbc + conv_b.astype(xbc.dtype)).astype(jnp.float32)
    gr = (SSM_GROUPS, SSM_REP)
    xs = xbc[..., :SSM_INNER].reshape(b, L, SSM_GROUPS, SSM_REP, SSM_HEAD_DIM)
    bm = xbc[..., SSM_INNER:SSM_INNER + SSM_GN].reshape(b, L, SSM_GROUPS, SSM_STATE)
    cm = xbc[..., SSM_INNER + SSM_GN:].reshape(b, L, SSM_GROUPS, SSM_STATE)
    dt = jax.nn.softplus(dt_raw.astype(jnp.float32).reshape(b, L, 2, SSM_HEADS)
                         + dt_bias.astype(jnp.float32))
    a = -jnp.exp(a_log.astype(jnp.float32))
    flip = lambda t: jnp.flip(t, axis=1)
    y_fwd = ssd_chunked(xs, dt[:, :, 0].reshape(b, L, *gr), a[0].reshape(gr), bm, cm)
    y_bwd = flip(ssd_chunked(flip(xs), flip(dt[:, :, 1]).reshape(b, L, *gr), a[1].reshape(gr),
                             flip(bm), flip(cm)))
    y = y_fwd + y_bwd + d_skip.astype(jnp.float32).reshape(gr)[..., None] * xs
    y = y.reshape(b, L, SSM_INNER) * jax.nn.silu(z.astype(jnp.float32))
    return rms_norm(y, ssm_norm).astype(z.dtype)


def mixer_block(x, norm_mix, w_in, conv_w, conv_b, attn_sink, a_log, dt_bias, d_skip,
                ssm_norm, w_br_attn, w_br_ssm, w_out):
    b, L, _ = x.shape
    u = rms_norm(x, norm_mix)
    proj = u @ w_in
    sizes = [ATT_WIDTH, KV_WIDTH, KV_WIDTH, SSM_INNER, CONV_DIM, 2 * SSM_HEADS, N_BRANCH * D_MODEL]
    cuts = [int(c) for c in np.cumsum(sizes)[:-1]]
    q, k, v, z, xbc, dt_raw, gate_raw = jnp.split(proj, cuts, axis=-1)
    y_attn = windowed_gqa(q, k, v, attn_sink) @ w_br_attn
    y_ssm = ssd_branch(xbc, z, dt_raw, conv_w, conv_b, a_log, dt_bias, d_skip, ssm_norm) @ w_br_ssm
    gates = jax.nn.sigmoid(gate_raw.astype(jnp.float32)).astype(x.dtype)
    merged = gates[..., :D_MODEL] * y_attn + gates[..., D_MODEL:] * y_ssm
    return merged @ w_out


def cross_attention(x, mem, norm_cross, norm_mem, w_q, w_kv, w_o):
    b, L, _ = x.shape
    M = mem.shape[1]
    u = rms_norm(x, norm_cross)
    mn = rms_norm(mem, norm_mem)
    q = (u @ w_q).reshape(b, L, X_HEADS, X_HEAD_DIM)
    kv = (mn @ w_kv).reshape(b, M, 2, X_HEADS, X_HEAD_DIM)
    k, v = kv[:, :, 0], kv[:, :, 1]
    s = jnp.einsum('bqhd,bkhd->bhqk', q, k).astype(jnp.float32) * (X_HEAD_DIM ** -0.5)
    p = jax.nn.softmax(s, axis=-1).astype(v.dtype)
    o = jnp.einsum('bhqk,bkhd->bqhd', p, v).reshape(b, L, X_WIDTH)
    return o @ w_o


def swiglu_ffn(x, norm_ffn, w_gate_up, w_down):
    u = rms_norm(x, norm_ffn)
    gu = u @ w_gate_up
    return (jax.nn.silu(gu[..., :FFN_HIDDEN]) * gu[..., FFN_HIDDEN:]) @ w_down


def trunk(x, mem, params):
    (norm_mix, w_in, conv_w, conv_b, attn_sink, a_log, dt_bias, d_skip, ssm_norm,
     w_br_attn, w_br_ssm, w_out, norm_cross, norm_mem, w_q_cross, w_kv_cross, w_o_cross,
     norm_ffn, w_gate_up, w_down, norm_final) = params
    for l in range(DEPTH):
        x = x + mixer_block(x, norm_mix[l], w_in[l], conv_w[l], conv_b[l], attn_sink[l], a_log[l],
                            dt_bias[l], d_skip[l], ssm_norm[l], w_br_attn[l], w_br_ssm[l], w_out[l])
        x = x + cross_attention(x, mem, norm_cross[l], norm_mem[l], w_q_cross[l], w_kv_cross[l],
                                w_o_cross[l])
        x = x + swiglu_ffn(x, norm_ffn[l], w_gate_up[l], w_down[l])
    return rms_norm(x, norm_final)


def setup_inputs(seed: int = 0) -> dict:
    key = jax.random.key(seed)
    ks = jax.random.split(key, 32)
    f32 = jnp.float32

    def nrm(k, shape, fan_in):
        return jax.random.normal(k, shape, f32) * (fan_in ** -0.5)

    def gain(k, shape):
        return 1.0 + 0.01 * jax.random.normal(k, shape, f32)

    u = jax.random.uniform(ks[8], (DEPTH, 2, SSM_HEADS), f32)
    dt0 = jnp.exp(u * (math.log(0.1) - math.log(0.001)) + math.log(0.001))
    dt_bias = dt0 + jnp.log(-jnp.expm1(-dt0))
    a_log = jnp.log(jax.random.uniform(ks[9], (DEPTH, 2, SSM_HEADS), f32, 1.0, 16.0))
    return {
        "x_prompt": jax.random.normal(ks[0], (BATCH, SEQ, D_MODEL), f32),
        "x_sample": jax.random.normal(ks[1], (DEC_BATCH, DEC_SEQ, D_MODEL), f32),
        "mem_prompt": jax.random.normal(ks[2], (BATCH, MEM_TOKENS, D_MODEL), f32),
        "mem_sample": jax.random.normal(ks[3], (DEC_BATCH, MEM_TOKENS, D_MODEL), f32),
        "norm_mix": gain(ks[4], (DEPTH, D_MODEL)),
        "w_in": nrm(ks[5], (DEPTH, D_MODEL, N_IN), D_MODEL),
        "conv_w": nrm(ks[6], (DEPTH, CONV_WIDTH, CONV_DIM), CONV_WIDTH),
        "conv_b": 0.01 * jax.random.normal(ks[7], (DEPTH, CONV_DIM), f32),
        "attn_sink": jax.random.normal(ks[10], (DEPTH, ATT_HEADS), f32),
        "a_log": a_log,
        "dt_bias": dt_bias,
        "d_skip": 1.0 + 0.1 * jax.random.normal(ks[11], (DEPTH, SSM_HEADS), f32),
        "ssm_norm": gain(ks[12], (DEPTH, SSM_INNER)),
        "w_br_attn": nrm(ks[13], (DEPTH, ATT_WIDTH, D_MODEL), ATT_WIDTH),
        "w_br_ssm": nrm(ks[14], (DEPTH, SSM_INNER, D_MODEL), SSM_INNER),
        "w_out": nrm(ks[15], (DEPTH, D_MODEL, D_MODEL), D_MODEL),
        "norm_cross": gain(ks[16], (DEPTH, D_MODEL)),
        "norm_mem": gain(ks[17], (DEPTH, D_MODEL)),
        "w_q_cross": nrm(ks[18], (DEPTH, D_MODEL, X_WIDTH), D_MODEL),
        "w_kv_cross": nrm(ks[19], (DEPTH, D_MODEL, 2 * X_WIDTH), D_MODEL),
        "w_o_cross": nrm(ks[20], (DEPTH, X_WIDTH, D_MODEL), X_WIDTH),
        "norm_ffn": gain(ks[21], (DEPTH, D_MODEL)),
        "w_gate_up": nrm(ks[22], (DEPTH, D_MODEL, 2 * FFN_HIDDEN), D_MODEL),
        "w_down": nrm(ks[23], (DEPTH, FFN_HIDDEN, D_MODEL), FFN_HIDDEN),
        "norm_final": gain(ks[24], (D_MODEL,)),
    }


def reference(x_prompt, x_sample, mem_prompt, mem_sample, norm_mix, w_in, conv_w, conv_b,
              attn_sink, a_log, dt_bias, d_skip, ssm_norm, w_br_attn, w_br_ssm, w_out,
              norm_cross, norm_mem, w_q_cross, w_kv_cross, w_o_cross, norm_ffn, w_gate_up,
              w_down, norm_final):
    params = (norm_mix, w_in, conv_w, conv_b, attn_sink, a_log, dt_bias, d_skip, ssm_norm,
              w_br_attn, w_br_ssm, w_out, norm_cross, norm_mem, w_q_cross, w_kv_cross, w_o_cross,
              norm_ffn, w_gate_up, w_down, norm_final)
    y_prompt = trunk(x_prompt, mem_prompt, params)
    y_sample = trunk(x_sample, mem_sample, params)
    return (y_prompt, y_sample)
```

```python
import functools

import numpy as np
import jax
import jax.numpy as jnp
from jax import lax
from jax.experimental import pallas as pl
from jax.experimental.pallas import tpu as pltpu

F32 = jnp.float32
BF16 = jnp.bfloat16

D_MODEL = 1024
EPS = 1e-6
ATT_HEADS = 16
ATT_KV_HEADS = 4
ATT_REP = ATT_HEADS // ATT_KV_HEADS
HEAD_DIM = 64
ATT_WIDTH = ATT_HEADS * HEAD_DIM
KV_WIDTH = ATT_KV_HEADS * HEAD_DIM
WINDOW = 128
SSM_INNER = 2 * D_MODEL
SSM_HEAD_DIM = 64
SSM_HEADS = SSM_INNER // SSM_HEAD_DIM
SSM_GROUPS = 4
SSM_REP = SSM_HEADS // SSM_GROUPS
SSM_STATE = 128
SSM_GN = SSM_GROUPS * SSM_STATE
CONV_WIDTH = 5
CONV_DIM = SSM_INNER + 2 * SSM_GN
CHUNK = 128
GROUP_WIDTH = SSM_REP * SSM_HEAD_DIM
X_HEADS = 4
X_HEAD_DIM = D_MODEL // X_HEADS
FFN_HIDDEN = -(-8 * D_MODEL // (3 * 256)) * 256

LANES = 128
BF16_SUBLANES = 16
VMEM_LIMIT_BYTES = 56 * 1024 * 1024

ROW_TILE = 512
ATT_TQ = 256
PROJ_COLS = 512


def _rms(x, g):
    return x * lax.rsqrt(jnp.mean(x * x, axis=-1, keepdims=True) + EPS) * g


def _const_spec(shape):
    nd = len(shape)
    return pl.BlockSpec(shape, lambda *_: (0,) * nd, pipeline_mode=pl.Buffered(1))


def _params(n_grid):
    return pltpu.CompilerParams(dimension_semantics=("arbitrary",) * n_grid,
                                vmem_limit_bytes=VMEM_LIMIT_BYTES)


def _dot(a, b):
    return jnp.dot(a, b, preferred_element_type=F32)


def _dot_nt(a, b):
    return lax.dot_general(a, b, (((1,), (1,)), ((), ())), preferred_element_type=F32)


def _dot_tn(a, b):
    return lax.dot_general(a, b, (((0,), (0,)), ((), ())), preferred_element_type=F32)


def _split3(v):
    h1 = v.astype(BF16)
    r1 = v - h1.astype(F32)
    h2 = r1.astype(BF16)
    r2 = r1 - h2.astype(F32)
    return h1, h2, r2.astype(BF16)


def _exact_dot(sel, v):
    h1, h2, h3 = _split3(v)
    return _dot(sel, h1) + _dot(sel, h2) + _dot(sel, h3)


def _expand(v, e2_ref):
    hi = v.astype(BF16)
    lo = (v - hi.astype(F32)).astype(BF16)
    return _dot(jnp.concatenate([hi, lo], axis=1), e2_ref[...])


def _inproj_kernel(x_ref, g_ref, wqkv_ref, wz_ref, wxbc_ref, wdt_ref, wgate_ref,
                   qkv_ref, z_ref, xbc_ref, dt_ref, gate_ref):
    u = _rms(x_ref[...], g_ref[...]).astype(BF16)

    def proj(w_ref, o_ref, post):
        n = w_ref.shape[1]
        for c0 in range(0, n, PROJ_COLS):
            c1 = min(c0 + PROJ_COLS, n)
            o_ref[:, c0:c1] = post(_dot(u, w_ref[:, c0:c1])).astype(o_ref.dtype)

    ident = lambda t: t
    proj(wqkv_ref, qkv_ref, ident)
    proj(wz_ref, z_ref, ident)
    proj(wxbc_ref, xbc_ref, ident)
    proj(wdt_ref, dt_ref, ident)
    proj(wgate_ref, gate_ref, jax.nn.sigmoid)


def _inproj(x2d, norm_mix, wqkv, wz, wxbc, wdt, wgate):
    t = x2d.shape[0]
    row = lambda n: pl.BlockSpec((ROW_TILE, n), lambda i: (i, 0))
    widths = (wqkv.shape[1], wz.shape[1], wxbc.shape[1], wdt.shape[1], wgate.shape[1])
    dtypes = (BF16, BF16, BF16, F32, BF16)
    return pl.pallas_call(
        _inproj_kernel,
        grid=(t // ROW_TILE,),
        in_specs=[row(D_MODEL), _const_spec((1, D_MODEL))]
        + [_const_spec(w.shape) for w in (wqkv, wz, wxbc, wdt, wgate)],
        out_specs=[row(n) for n in widths],
        out_shape=[jax.ShapeDtypeStruct((t, n), dt) for n, dt in zip(widths, dtypes)],
        compiler_params=_params(1),
        name="inproj",
    )(x2d, norm_mix, wqkv, wz, wxbc, wdt, wgate)


def _attn_kernel(seq_len, q_ref, kp_ref, kc_ref, kn_ref, vp_ref, vc_ref, vn_ref, slope_ref, sink_ref,
                 wbr_ref, gate_ref, out_ref, o_scr):
    i = pl.program_id(1)
    tk = ATT_TQ + 2 * WINDOW
    r = lax.broadcasted_iota(jnp.int32, (ATT_TQ, tk), 0)
    j = lax.broadcasted_iota(jnp.int32, (ATT_TQ, tk), 1)
    dist = jnp.abs(r + WINDOW - j)
    kpos = i * ATT_TQ - WINDOW + j
    valid = (dist <= WINDOW) & (kpos >= 0) & (kpos < seq_len)
    distf = dist.astype(F32)
    k = jnp.concatenate([kp_ref[0], kc_ref[0], kn_ref[0]], axis=0)
    v = jnp.concatenate([vp_ref[0], vc_ref[0], vn_ref[0]], axis=0)
    scale = HEAD_DIM ** -0.5
    for g in range(ATT_KV_HEADS):
        kg = k[:, g * HEAD_DIM:(g + 1) * HEAD_DIM]
        vg = v[:, g * HEAD_DIM:(g + 1) * HEAD_DIM]
        for rr in range(ATT_REP):
            h = g * ATT_REP + rr
            qh = q_ref[0, :, h * HEAD_DIM:(h + 1) * HEAD_DIM]
            s = _dot_nt(qh, kg) * scale - slope_ref[h] * distf
            s = jnp.where(valid, s, -jnp.inf)
            sink = sink_ref[h]
            m = jnp.maximum(jnp.max(s, axis=-1, keepdims=True), sink)
            p = jnp.exp(s - m)
            denom = jnp.sum(p, axis=-1, keepdims=True) + jnp.exp(sink - m)
            o = _dot(p.astype(BF16), vg) / denom
            o_scr[:, h * HEAD_DIM:(h + 1) * HEAD_DIM] = o.astype(BF16)
    y = _dot(o_scr[...], wbr_ref[...])
    out_ref[0] = (gate_ref[0].astype(F32) * y).astype(out_ref.dtype)


def _attention(qkv, gate, slopes, sink, w_br_attn, b, seq_len):
    nq = seq_len // ATT_TQ
    halo_per_q = ATT_TQ // WINDOW
    n_halo = seq_len // WINDOW
    k_col, v_col = ATT_WIDTH // KV_WIDTH, ATT_WIDTH // KV_WIDTH + 1
    prev = lambda col: pl.BlockSpec(
        (1, WINDOW, KV_WIDTH), lambda bi, i: (bi, jnp.maximum(i * halo_per_q - 1, 0), col))
    cur = lambda col: pl.BlockSpec((1, ATT_TQ, KV_WIDTH), lambda bi, i: (bi, i, col))
    nxt = lambda col: pl.BlockSpec(
        (1, WINDOW, KV_WIDTH), lambda bi, i: (bi, jnp.minimum((i + 1) * halo_per_q, n_halo - 1), col))
    smem = pl.BlockSpec(memory_space=pltpu.SMEM)
    return pl.pallas_call(
        functools.partial(_attn_kernel, seq_len),
        grid=(b, nq),
        in_specs=[pl.BlockSpec((1, ATT_TQ, ATT_WIDTH), lambda bi, i: (bi, i, 0)),
                  prev(k_col), cur(k_col), nxt(k_col), prev(v_col), cur(v_col), nxt(v_col),
                  smem, smem, _const_spec(w_br_attn.shape),
                  pl.BlockSpec((1, ATT_TQ, D_MODEL), lambda bi, i: (bi, i, 0))],
        out_specs=pl.BlockSpec((1, ATT_TQ, D_MODEL), lambda bi, i: (bi, i, 0)),
        out_shape=jax.ShapeDtypeStruct((b, seq_len, D_MODEL), BF16),
        scratch_shapes=[pltpu.VMEM((ATT_TQ, ATT_WIDTH), BF16)],
        compiler_params=_params(2),
        name="window_attn",
    )(qkv, qkv, qkv, qkv, qkv, qkv, qkv, slopes, sink, w_br_attn, gate)


CONV_HALO = CONV_WIDTH // 2
CONV_PAD = 8


def _dt_terms(dt_ref, dtb_ref, a_ref, tril_ref, triu_ref):
    dt = jax.nn.softplus(dt_ref[0] + dtb_ref[...])
    dta = dt * a_ref[...]
    h1, h2, h3 = _split3(dta)
    pre = _dot(tril_ref[...], h1) + _dot(tril_ref[...], h2) + _dot(tril_ref[...], h3)
    suf = _dot(triu_ref[...], h1) + _dot(triu_ref[...], h2) + _dot(triu_ref[...], h3)
    lane = lax.broadcasted_iota(jnp.int32, dt.shape, 1)
    cum = jnp.where(lane < SSM_HEADS, pre, suf)
    return dt, cum


def _ssd_fwd_kernel(xp_ref, xc_ref, xn_ref, cw_ref, cb_ref, dt_ref, dtb_ref, a_ref, tril_ref, triu_ref,
                    e2_ref, xconv_ref, sin_ref, ext_scr, state_scr):
    c = pl.program_id(1)
    nc = pl.num_programs(1)

    @pl.when(c == 0)
    def _():
        state_scr[...] = jnp.zeros_like(state_scr)

    has_prev = (c > 0).astype(F32)
    has_next = (c < nc - 1).astype(F32)
    ext_scr[0:CONV_PAD, :] = xp_ref[0, BF16_SUBLANES - CONV_PAD:, :].astype(F32) * has_prev
    ext_scr[CONV_PAD:CONV_PAD + CHUNK, :] = xc_ref[0].astype(F32)
    ext_scr[CONV_PAD + CHUNK:, :] = xn_ref[0, :CONV_PAD, :].astype(F32) * has_next
    for c0 in range(0, CONV_DIM, PROJ_COLS):
        acc = jnp.zeros((CHUNK, PROJ_COLS), F32) + cb_ref[:, c0:c0 + PROJ_COLS]
        for kk in range(CONV_WIDTH):
            acc = acc + (ext_scr[pl.ds(CONV_PAD - CONV_HALO + kk, CHUNK), c0:c0 + PROJ_COLS]
                         * cw_ref[kk:kk + 1, c0:c0 + PROJ_COLS])
        xconv_ref[0, :, c0:c0 + PROJ_COLS] = (acc * jax.nn.sigmoid(acc)).astype(BF16)

    dt, cum = _dt_terms(dt_ref, dtb_ref, a_ref, tril_ref, triu_ref)
    total = cum[CHUNK - 1:CHUNK, :]
    fwd_lane = lax.broadcasted_iota(jnp.int32, dt.shape, 1) < SSM_HEADS
    w = jnp.exp(jnp.where(fwd_lane, total - cum, 0.0)) * dt
    wexp = _expand(w, e2_ref)
    cdec = _expand(jnp.broadcast_to(jnp.exp(total), (8, 2 * SSM_HEADS)), e2_ref)[0:1, :]
    for g in range(SSM_GROUPS):
        lo, hi = g * GROUP_WIDTH, (g + 1) * GROUP_WIDTH
        xw = (xconv_ref[0, :, lo:hi].astype(F32) * wexp[:, lo:hi]).astype(BF16)
        bg = xconv_ref[0, :, SSM_INNER + g * SSM_STATE:SSM_INNER + (g + 1) * SSM_STATE]
        st = state_scr[g]
        sin_ref[0, 0, g] = st.astype(BF16)
        state_scr[g] = st * cdec[:, lo:hi] + _dot_tn(bg, xw)


def _ssd_main_kernel(xc_ref, dt_ref, z_ref, sin_ref, attn_ref, gate_ref, xres_ref,
                     dtb_ref, a_ref, tril_ref, triu_ref, e2f_ref, e2b_ref, dskip_ref, snorm_ref,
                     wbr_ref, wout_ref, out_ref, y_scr, state_scr):
    c = pl.program_id(1)

    @pl.when(c == 0)
    def _():
        state_scr[...] = jnp.zeros_like(state_scr)

    dt, cum = _dt_terms(dt_ref, dtb_ref, a_ref, tril_ref, triu_ref)
    pad = jnp.zeros((CHUNK, LANES - 2 * SSM_HEADS), F32)
    cum_t = jnp.concatenate([cum, pad], axis=1).T
    dt_t = jnp.concatenate([dt, pad], axis=1).T
    li = lax.broadcasted_iota(jnp.int32, (CHUNK, CHUNK), 0)
    si = lax.broadcasted_iota(jnp.int32, (CHUNK, CHUNK), 1)
    lower = li >= si
    upper = li <= si
    lane = lax.broadcasted_iota(jnp.int32, (CHUNK, LANES), 1)
    first_half = lane < SSM_HEAD_DIM

    for g in range(SSM_GROUPS):
        bg = xc_ref[0, :, SSM_INNER + g * SSM_STATE:SSM_INNER + (g + 1) * SSM_STATE]
        cg = xc_ref[0, :, SSM_INNER + SSM_GN + g * SSM_STATE:SSM_INNER + SSM_GN + (g + 1) * SSM_STATE]
        cb = _dot_nt(cg, bg)
        for pr in range(SSM_REP // 2):
            ms = []
            for hh in range(2):
                h = g * SSM_REP + 2 * pr + hh
                hb = SSM_HEADS + h
                seg_f = cum[:, h:h + 1] - cum_t[h:h + 1, :]
                seg_b = cum[:, hb:hb + 1] - cum_t[hb:hb + 1, :]
                dec = (jnp.exp(jnp.where(lower, seg_f, -jnp.inf)) * dt_t[h:h + 1, :]
                       + jnp.exp(jnp.where(upper, seg_b, -jnp.inf)) * dt_t[hb:hb + 1, :])
                ms.append((cb * dec).astype(BF16))
            col = (g * SSM_REP + 2 * pr) * SSM_HEAD_DIM
            xpair = xc_ref[0, :, col:col + LANES]
            zero = jnp.zeros_like(xpair)
            xdiag = jnp.concatenate([jnp.where(first_half, xpair, zero),
                                     jnp.where(first_half, zero, xpair)], axis=0)
            y_scr[:, col:col + LANES] = _dot(jnp.concatenate(ms, axis=1), xdiag)

    ef = _expand(jnp.exp(cum), e2f_ref)
    eb = _expand(jnp.exp(cum), e2b_ref)
    first = cum[0:1, :]
    bwd_lane = lax.broadcasted_iota(jnp.int32, dt.shape, 1) >= SSM_HEADS
    wb = _expand(jnp.exp(jnp.where(bwd_lane, first - cum, 0.0)) * dt, e2b_ref)
    cdec = _expand(jnp.broadcast_to(jnp.exp(first), (8, 2 * SSM_HEADS)), e2b_ref)[0:1, :]
    for g in range(SSM_GROUPS):
        lo, hi = g * GROUP_WIDTH, (g + 1) * GROUP_WIDTH
        bg = xc_ref[0, :, SSM_INNER + g * SSM_STATE:SSM_INNER + (g + 1) * SSM_STATE]
        cg = xc_ref[0, :, SSM_INNER + SSM_GN + g * SSM_STATE:SSM_INNER + SSM_GN + (g + 1) * SSM_STATE]
        st = state_scr[g]
        xg = xc_ref[0, :, lo:hi].astype(F32)
        y = (y_scr[:, lo:hi] + _dot(cg, sin_ref[0, 0, g]) * ef[:, lo:hi]
             + _dot(cg, st.astype(BF16)) * eb[:, lo:hi] + dskip_ref[:, lo:hi] * xg)
        zg = z_ref[0, :, lo:hi].astype(F32)
        y_scr[:, lo:hi] = y * (zg * jax.nn.sigmoid(zg))
        xw = (xg * wb[:, lo:hi]).astype(BF16)
        state_scr[g] = st * cdec[:, lo:hi] + _dot_tn(bg, xw)

    yn = _rms(y_scr[...], snorm_ref[...]).astype(BF16)
    y_ssm = _dot(yn, wbr_ref[...])
    merged = attn_ref[0].astype(F32) + gate_ref[0].astype(F32) * y_ssm
    out_ref[0] = xres_ref[0] + _dot(merged.astype(BF16), wout_ref[...])


def _ssd_constants():
    idx = np.arange(CHUNK)
    tril = (idx[:, None] >= idx[None, :]).astype(np.float32)
    heads = np.arange(2 * SSM_HEADS)
    chan_head = np.arange(SSM_INNER) // SSM_HEAD_DIM
    e_f = (heads[:, None] == chan_head[None, :]).astype(np.float32)
    e_b = (heads[:, None] == chan_head[None, :] + SSM_HEADS).astype(np.float32)
    stack = lambda e: jnp.asarray(np.concatenate([e, e], axis=0), BF16)
    return jnp.asarray(tril, BF16), jnp.asarray(tril.T, BF16), stack(e_f), stack(e_b)


def _ssd_fwd(xbc, dt_raw, conv_w, conv_b, dt_bias, a_neg, consts, b, seq_len):
    tril, triu, e2f, _ = consts
    nc = seq_len // CHUNK
    halo_blocks = CHUNK // BF16_SUBLANES
    n_halo = seq_len // BF16_SUBLANES
    return pl.pallas_call(
        _ssd_fwd_kernel,
        grid=(b, nc),
        in_specs=[pl.BlockSpec((1, BF16_SUBLANES, CONV_DIM),
                               lambda bi, c: (bi, jnp.maximum(c * halo_blocks - 1, 0), 0)),
                  pl.BlockSpec((1, CHUNK, CONV_DIM), lambda bi, c: (bi, c, 0)),
                  pl.BlockSpec((1, BF16_SUBLANES, CONV_DIM),
                               lambda bi, c: (bi, jnp.minimum((c + 1) * halo_blocks, n_halo - 1), 0)),
                  _const_spec(conv_w.shape), _const_spec(conv_b.shape),
                  pl.BlockSpec((1, CHUNK, 2 * SSM_HEADS), lambda bi, c: (bi, c, 0)),
                  _const_spec(dt_bias.shape), _const_spec(a_neg.shape),
                  _const_spec(tril.shape), _const_spec(triu.shape), _const_spec(e2f.shape)],
        out_specs=[pl.BlockSpec((1, CHUNK, CONV_DIM), lambda bi, c: (bi, c, 0)),
                   pl.BlockSpec((1, 1, SSM_GROUPS, SSM_STATE, GROUP_WIDTH), lambda bi, c: (bi, c, 0, 0, 0))],
        out_shape=[jax.ShapeDtypeStruct((b, seq_len, CONV_DIM), BF16),
                   jax.ShapeDtypeStruct((b, nc, SSM_GROUPS, SSM_STATE, GROUP_WIDTH), BF16)],
        scratch_shapes=[pltpu.VMEM((CONV_PAD + CHUNK + CONV_PAD, CONV_DIM), F32),
                        pltpu.VMEM((SSM_GROUPS, SSM_STATE, GROUP_WIDTH), F32)],
        compiler_params=_params(2),
        name="ssd_fwd",
    )(xbc, xbc, xbc, conv_w, conv_b, dt_raw, dt_bias, a_neg, tril, triu, e2f)


def _ssd_main(xconv, dt_raw, z, s_in, attn_part, gate, x3d, dt_bias, a_neg, consts, d_skip_exp, ssm_norm,
              w_br_ssm, w_out, b, seq_len):
    tril, triu, e2f, e2b = consts
    nc = seq_len // CHUNK
    rev = lambda width, col=0: pl.BlockSpec((1, CHUNK, width), lambda bi, c: (bi, nc - 1 - c, col))
    return pl.pallas_call(
        _ssd_main_kernel,
        grid=(b, nc),
        in_specs=[rev(CONV_DIM), rev(2 * SSM_HEADS), rev(SSM_INNER),
                  pl.BlockSpec((1, 1, SSM_GROUPS, SSM_STATE, GROUP_WIDTH),
                               lambda bi, c: (bi, nc - 1 - c, 0, 0, 0)),
                  rev(D_MODEL), rev(D_MODEL, 1), rev(D_MODEL)]
        + [_const_spec(a.shape) for a in (dt_bias, a_neg, tril, triu, e2f, e2b, d_skip_exp, ssm_norm,
                                          w_br_ssm, w_out)],
        out_specs=rev(D_MODEL),
        out_shape=jax.ShapeDtypeStruct((b, seq_len, D_MODEL), F32),
        scratch_shapes=[pltpu.VMEM((CHUNK, SSM_INNER), F32),
                        pltpu.VMEM((SSM_GROUPS, SSM_STATE, GROUP_WIDTH), F32)],
        compiler_params=_params(2),
        name="ssd_main",
    )(xconv, dt_raw, z, s_in, attn_part, gate, x3d, dt_bias, a_neg, tril, triu, e2f, e2b, d_skip_exp,
      ssm_norm, w_br_ssm, w_out)


def _memkv_kernel(mem_ref, g_ref, wkv_ref, kv_ref):
    mn = _rms(mem_ref[0], g_ref[...]).astype(BF16)
    for c0 in range(0, 2 * D_MODEL, PROJ_COLS):
        kv_ref[0, :, c0:c0 + PROJ_COLS] = _dot(mn, wkv_ref[:, c0:c0 + PROJ_COLS]).astype(BF16)


def _memkv(mem, norm_mem, w_kv):
    b, m, _ = mem.shape
    return pl.pallas_call(
        _memkv_kernel,
        grid=(b,),
        in_specs=[pl.BlockSpec((1, m, D_MODEL), lambda bi: (bi, 0, 0)),
                  _const_spec(norm_mem.shape), _const_spec(w_kv.shape)],
        out_specs=pl.BlockSpec((1, m, 2 * D_MODEL), lambda bi: (bi, 0, 0)),
        out_shape=jax.ShapeDtypeStruct((b, m, 2 * D_MODEL), BF16),
        compiler_params=_params(1),
        name="mem_kv",
    )(mem, norm_mem, w_kv)


def _tail_kernel(x_ref, kv_ref, ncross_ref, wq_ref, wo_ref, nffn_ref, wg_ref, wu_ref, wd_ref, nfin_ref,
                 out_ref, o_scr, h_scr):
    x = x_ref[0]
    u = _rms(x, ncross_ref[...]).astype(BF16)
    scale = X_HEAD_DIM ** -0.5
    for h in range(X_HEADS):
        lo, hi = h * X_HEAD_DIM, (h + 1) * X_HEAD_DIM
        q = _dot(u, wq_ref[:, lo:hi]).astype(BF16)
        s = _dot_nt(q, kv_ref[0, :, lo:hi]) * scale
        p = jnp.exp(s - jnp.max(s, axis=-1, keepdims=True))
        denom = jnp.sum(p, axis=-1, keepdims=True)
        o = _dot(p.astype(BF16), kv_ref[0, :, D_MODEL + lo:D_MODEL + hi]) / denom
        o_scr[:, lo:hi] = o.astype(BF16)
    x = x + _dot(o_scr[...], wo_ref[...])
    u = _rms(x, nffn_ref[...]).astype(BF16)
    for c0 in range(0, FFN_HIDDEN, PROJ_COLS):
        c1 = min(c0 + PROJ_COLS, FFN_HIDDEN)
        gte = _dot(u, wg_ref[:, c0:c1])
        up = _dot(u, wu_ref[:, c0:c1])
        h_scr[:, c0:c1] = (gte * jax.nn.sigmoid(gte) * up).astype(BF16)
    x = x + _dot(h_scr[...], wd_ref[...])
    out_ref[0] = _rms(x, nfin_ref[...])


def _tail(x3d, kv, norm_cross, w_q, w_o, norm_ffn, w_gate, w_up, w_down, norm_final):
    b, seq_len, _ = x3d.shape
    m = kv.shape[1]
    consts = (norm_cross, w_q, w_o, norm_ffn, w_gate, w_up, w_down, norm_final)
    return pl.pallas_call(
        _tail_kernel,
        grid=(b, seq_len // ROW_TILE),
        in_specs=[pl.BlockSpec((1, ROW_TILE, D_MODEL), lambda bi, i: (bi, i, 0)),
                  pl.BlockSpec((1, m, 2 * D_MODEL), lambda bi, i: (bi, 0, 0))]
        + [_const_spec(a.shape) for a in consts],
        out_specs=pl.BlockSpec((1, ROW_TILE, D_MODEL), lambda bi, i: (bi, i, 0)),
        out_shape=jax.ShapeDtypeStruct((b, seq_len, D_MODEL), F32),
        scratch_shapes=[pltpu.VMEM((ROW_TILE, D_MODEL), BF16), pltpu.VMEM((ROW_TILE, FFN_HIDDEN), BF16)],
        compiler_params=_params(2),
        name="cross_ffn",
    )(x3d, kv, *consts)


def _prepare(norm_mix, w_in, conv_w, conv_b, attn_sink, a_log, dt_bias, d_skip, ssm_norm, w_br_attn,
             w_br_ssm, w_out, norm_cross, norm_mem, w_q_cross, w_kv_cross, w_o_cross, norm_ffn, w_gate_up,
             w_down, norm_final):
    row = lambda v: v.reshape(1, -1).astype(F32)
    w = w_in[0]
    cuts = np.cumsum([0, ATT_WIDTH + 2 * KV_WIDTH, SSM_INNER, CONV_DIM, 2 * SSM_HEADS, 2 * D_MODEL])
    wqkv, wz, wxbc, wdt, wgate = (w[:, int(s):int(e)].astype(BF16) for s, e in zip(cuts[:-1], cuts[1:]))
    return dict(
        norm_mix=row(norm_mix[0]), wqkv=wqkv, wz=wz, wxbc=wxbc, wdt=wdt, wgate=wgate,
        conv_w=conv_w[0].astype(F32), conv_b=row(conv_b[0]),
        slopes=jnp.exp2(-8.0 * jnp.arange(1, ATT_HEADS + 1, dtype=F32) / ATT_HEADS),
        sink=attn_sink[0].astype(F32),
        a_neg=row(-jnp.exp(a_log[0].astype(F32))), dt_bias=row(dt_bias[0]),
        d_skip=row(jnp.repeat(d_skip[0].astype(F32), SSM_HEAD_DIM)), ssm_norm=row(ssm_norm[0]),
        w_br_attn=w_br_attn[0].astype(BF16), w_br_ssm=w_br_ssm[0].astype(BF16), w_out=w_out[0].astype(BF16),
        norm_cross=row(norm_cross[0]), norm_mem=row(norm_mem[0]),
        w_q=w_q_cross[0].astype(BF16), w_kv=w_kv_cross[0].astype(BF16), w_o=w_o_cross[0].astype(BF16),
        norm_ffn=row(norm_ffn[0]), w_gate=w_gate_up[0][:, :FFN_HIDDEN].astype(BF16),
        w_up=w_gate_up[0][:, FFN_HIDDEN:].astype(BF16), w_down=w_down[0].astype(BF16),
        norm_final=row(norm_final), consts=_ssd_constants())


def _trunk(x, mem, p):
    b, seq_len, _ = x.shape
    assert seq_len % ROW_TILE == 0 and seq_len % ATT_TQ == 0 and seq_len % CHUNK == 0
    qkv, z, xbc, dt_raw, gate = _inproj(x.reshape(b * seq_len, D_MODEL), p["norm_mix"], p["wqkv"], p["wz"],
                                        p["wxbc"], p["wdt"], p["wgate"])
    as3d = lambda t: t.reshape(b, seq_len, t.shape[-1])
    qkv, z, xbc, dt_raw, gate = as3d(qkv), as3d(z), as3d(xbc), as3d(dt_raw), as3d(gate)
    attn_part = _attention(qkv, gate, p["slopes"], p["sink"], p["w_br_attn"], b, seq_len)
    xconv, s_in = _ssd_fwd(xbc, dt_raw, p["conv_w"], p["conv_b"], p["dt_bias"], p["a_neg"], p["consts"],
                           b, seq_len)
    x1 = _ssd_main(xconv, dt_raw, z, s_in, attn_part, gate, x, p["dt_bias"], p["a_neg"], p["consts"],
                   p["d_skip"], p["ssm_norm"], p["w_br_ssm"], p["w_out"], b, seq_len)
    kv = _memkv(mem, p["norm_mem"], p["w_kv"])
    return _tail(x1, kv, p["norm_cross"], p["w_q"], p["w_o"], p["norm_ffn"], p["w_gate"], p["w_up"],
                 p["w_down"], p["norm_final"])


def kernel(x_prompt, x_sample, mem_prompt, mem_sample, norm_mix, w_in, conv_w, conv_b, attn_sink, a_log,
           dt_bias, d_skip, ssm_norm, w_br_attn, w_br_ssm, w_out, norm_cross, norm_mem, w_q_cross,
           w_kv_cross, w_o_cross, norm_ffn, w_gate_up, w_down, norm_final):
    p = _prepare(norm_mix, w_in, conv_w, conv_b, attn_sink, a_log, dt_bias, d_skip, ssm_norm, w_br_attn,
                 w_br_ssm, w_out, norm_cross, norm_mem, w_q_cross, w_kv_cross, w_o_cross, norm_ffn,
                 w_gate_up, w_down, norm_final)
    return (_trunk(x_prompt, mem_prompt, p), _trunk(x_sample, mem_sample, p))
```

```python
import functools

import numpy as np
import jax
import jax.numpy as jnp
from jax import lax
from jax.experimental import pallas as pl
from jax.experimental.pallas import tpu as pltpu

F32 = jnp.float32
BF16 = jnp.bfloat16

D_MODEL = 1024
EPS = 1e-6
ATT_HEADS = 16
ATT_KV_HEADS = 4
ATT_REP = ATT_HEADS // ATT_KV_HEADS
HEAD_DIM = 64
ATT_WIDTH = ATT_HEADS * HEAD_DIM
KV_WIDTH = ATT_KV_HEADS * HEAD_DIM
WINDOW = 128
SSM_INNER = 2 * D_MODEL
SSM_HEAD_DIM = 64
SSM_HEADS = SSM_INNER // SSM_HEAD_DIM
SSM_GROUPS = 4
SSM_REP = SSM_HEADS // SSM_GROUPS
SSM_STATE = 128
SSM_GN = SSM_GROUPS * SSM_STATE
CONV_WIDTH = 5
CONV_DIM = SSM_INNER + 2 * SSM_GN
CHUNK = 128
GROUP_WIDTH = SSM_REP * SSM_HEAD_DIM
X_HEADS = 4
X_HEAD_DIM = D_MODEL // X_HEADS
FFN_HIDDEN = -(-8 * D_MODEL // (3 * 256)) * 256

LANES = 128
BF16_SUBLANES = 16
VMEM_LIMIT_BYTES = 56 * 1024 * 1024

ROW_TILE = 512
ATT_TQ = 256
ATT_SKEW = 2
PROJ_COLS = 512

LOG2E = 1.4426950408889634
Q_SCALE = HEAD_DIM ** -0.5 * LOG2E


def _rms(x, g):
    return x * lax.rsqrt(jnp.mean(x * x, axis=-1, keepdims=True) + EPS) * g


def _const_spec(shape):
    nd = len(shape)
    return pl.BlockSpec(shape, lambda *_: (0,) * nd, pipeline_mode=pl.Buffered(1))


def _params(n_grid):
    return pltpu.CompilerParams(dimension_semantics=("arbitrary",) * n_grid,
                                vmem_limit_bytes=VMEM_LIMIT_BYTES)


def _dot(a, b):
    return jnp.dot(a, b, preferred_element_type=F32)


def _dot_nt(a, b):
    return lax.dot_general(a, b, (((1,), (1,)), ((), ())), preferred_element_type=F32)


def _dot_tn(a, b):
    return lax.dot_general(a, b, (((0,), (0,)), ((), ())), preferred_element_type=F32)


def _split3(v):
    h1 = v.astype(BF16)
    r1 = v - h1.astype(F32)
    h2 = r1.astype(BF16)
    r2 = r1 - h2.astype(F32)
    return h1, h2, r2.astype(BF16)


def _exact_dot(sel, v):
    h1, h2, h3 = _split3(v)
    return _dot(sel, h1) + _dot(sel, h2) + _dot(sel, h3)


def _expand(v, e2_ref):
    hi = v.astype(BF16)
    lo = (v - hi.astype(F32)).astype(BF16)
    return _dot(jnp.concatenate([hi, lo], axis=1), e2_ref[...])


def _inproj_kernel(x_ref, g_ref, wqkv_ref, wz_ref, wxbc_ref, wdt_ref, wgate_ref,
                   qkv_ref, z_ref, xbc_ref, dt_ref, gate_ref):
    u = _rms(x_ref[...], g_ref[...]).astype(BF16)

    def proj(w_ref, o_ref, post):
        n = w_ref.shape[1]
        for c0 in range(0, n, PROJ_COLS):
            c1 = min(c0 + PROJ_COLS, n)
            o_ref[:, c0:c1] = post(_dot(u, w_ref[:, c0:c1])).astype(o_ref.dtype)

    ident = lambda t: t
    for c0 in range(0, ATT_WIDTH, PROJ_COLS):
        qkv_ref[:, c0:c0 + PROJ_COLS] = (_dot(u, wqkv_ref[:, c0:c0 + PROJ_COLS]) * Q_SCALE).astype(BF16)
    qkv_ref[:, ATT_WIDTH:] = _dot(u, wqkv_ref[:, ATT_WIDTH:]).astype(BF16)
    proj(wz_ref, z_ref, ident)
    proj(wxbc_ref, xbc_ref, ident)
    proj(wdt_ref, dt_ref, ident)
    proj(wgate_ref, gate_ref, jax.nn.sigmoid)


def _inproj(x2d, norm_mix, wqkv, wz, wxbc, wdt, wgate):
    t = x2d.shape[0]
    row = lambda n: pl.BlockSpec((ROW_TILE, n), lambda i: (i, 0))
    widths = (wqkv.shape[1], wz.shape[1], wxbc.shape[1], wdt.shape[1], wgate.shape[1])
    dtypes = (BF16, BF16, BF16, F32, BF16)
    return pl.pallas_call(
        _inproj_kernel,
        grid=(t // ROW_TILE,),
        in_specs=[row(D_MODEL), _const_spec((1, D_MODEL))]
        + [_const_spec(w.shape) for w in (wqkv, wz, wxbc, wdt, wgate)],
        out_specs=[row(n) for n in widths],
        out_shape=[jax.ShapeDtypeStruct((t, n), dt) for n, dt in zip(widths, dtypes)],
        compiler_params=_params(1),
        name="inproj",
    )(x2d, norm_mix, wqkv, wz, wxbc, wdt, wgate)


def _attn_kernel(seq_len, q_ref, kp_ref, kc_ref, kn_ref, vp_ref, vc_ref, vn_ref, half_ref, slope_ref,
                 sink_ref, wbr_ref, gate_ref, out_ref, ot_scr, s_scr):
    i = pl.program_id(1)
    tk = ATT_TQ + 2 * WINDOW
    j = lax.broadcasted_iota(jnp.int32, (tk, ATT_TQ), 0)
    r = lax.broadcasted_iota(jnp.int32, (tk, ATT_TQ), 1)
    dist = jnp.abs(r + WINDOW - j)
    kpos = i * ATT_TQ - WINDOW + j
    valid = (dist <= WINDOW) & (kpos >= 0) & (kpos < seq_len)
    mdist = jnp.where(valid, dist.astype(F32), jnp.inf)
    k = jnp.concatenate([kp_ref[0], kc_ref[0], kn_ref[0]], axis=0)
    v = jnp.concatenate([vp_ref[0], vc_ref[0], vn_ref[0]], axis=0)
    k_halves, v2t = [], []
    for pair in range(ATT_KV_HEADS // 2):
        k2 = k[:, pair * LANES:(pair + 1) * LANES]
        k_halves.append((k2 * half_ref[0:1, :], k2 * half_ref[1:2, :]))
        v2t.append(v[:, pair * LANES:(pair + 1) * LANES].T)

    def logits(pos):
        tile, half = pos // 2, pos % 2
        qt = q_ref[0, :, tile * LANES:(tile + 1) * LANES]
        s_scr[pos] = _dot_nt(k_halves[tile // ATT_REP][half], qt) - slope_ref[pos] * mdist

    def attend(pos):
        tile, half = pos // 2, pos % 2
        s = s_scr[pos]
        sink = sink_ref[pos]
        m = jnp.maximum(jnp.max(s, axis=0, keepdims=True), sink)
        p = jnp.exp2(s - m)
        denom = jnp.sum(p, axis=0, keepdims=True) + jnp.exp2(sink - m)
        vt = v2t[tile // ATT_REP][half * HEAD_DIM:(half + 1) * HEAD_DIM]
        ot_scr[pos * HEAD_DIM:(pos + 1) * HEAD_DIM, :] = (_dot(vt, p.astype(BF16)) / denom).astype(BF16)

    for pos in range(ATT_HEADS + ATT_SKEW):
        if pos < ATT_HEADS:
            logits(pos)
        if pos >= ATT_SKEW:
            attend(pos - ATT_SKEW)
    y = _dot_tn(ot_scr[...], wbr_ref[...])
    out_ref[0] = (gate_ref[0].astype(F32) * y).astype(out_ref.dtype)


_ATT_HEAD_ORDER = [ATT_REP * (2 * (p // (2 * ATT_REP)) + p % 2) + (p // 2) % ATT_REP for p in range(ATT_HEADS)]


def _attention(qkv, gate, slopes, sink, w_br_attn, b, seq_len):
    nq = seq_len // ATT_TQ
    halo_per_q = ATT_TQ // WINDOW
    n_halo = seq_len // WINDOW
    k_col, v_col = ATT_WIDTH // KV_WIDTH, ATT_WIDTH // KV_WIDTH + 1
    prev = lambda col: pl.BlockSpec(
        (1, WINDOW, KV_WIDTH), lambda bi, i: (bi, jnp.maximum(i * halo_per_q - 1, 0), col))
    cur = lambda col: pl.BlockSpec((1, ATT_TQ, KV_WIDTH), lambda bi, i: (bi, i, col))
    nxt = lambda col: pl.BlockSpec(
        (1, WINDOW, KV_WIDTH), lambda bi, i: (bi, jnp.minimum((i + 1) * halo_per_q, n_halo - 1), col))
    smem = pl.BlockSpec(memory_space=pltpu.SMEM)
    lane_half = np.arange(LANES) // HEAD_DIM
    half_masks = jnp.asarray(np.stack([lane_half == 0, lane_half == 1]), BF16)
    return pl.pallas_call(
        functools.partial(_attn_kernel, seq_len),
        grid=(b, nq),
        in_specs=[pl.BlockSpec((1, ATT_TQ, ATT_WIDTH), lambda bi, i: (bi, i, 0)),
                  prev(k_col), cur(k_col), nxt(k_col), prev(v_col), cur(v_col), nxt(v_col),
                  _const_spec(half_masks.shape), smem, smem, _const_spec(w_br_attn.shape),
                  pl.BlockSpec((1, ATT_TQ, D_MODEL), lambda bi, i: (bi, i, 0))],
        out_specs=pl.BlockSpec((1, ATT_TQ, D_MODEL), lambda bi, i: (bi, i, 0)),
        out_shape=jax.ShapeDtypeStruct((b, seq_len, D_MODEL), BF16),
        scratch_shapes=[pltpu.VMEM((ATT_WIDTH, ATT_TQ), BF16),
                        pltpu.VMEM((ATT_HEADS, ATT_TQ + 2 * WINDOW, ATT_TQ), F32)],
        compiler_params=_params(2),
        name="window_attn",
    )(qkv, qkv, qkv, qkv, qkv, qkv, qkv, half_masks, slopes, sink, w_br_attn, gate)


CONV_HALO = CONV_WIDTH // 2


def _dt_terms(dt_ref, dtb_ref, a_ref, tril_ref, triu_ref):
    dt = jax.nn.softplus(dt_ref[0] + dtb_ref[...])
    dta = dt * a_ref[...]
    h1, h2, h3 = _split3(dta)
    pre = _dot(tril_ref[...], h1) + _dot(tril_ref[...], h2) + _dot(tril_ref[...], h3)
    suf = _dot(triu_ref[...], h1) + _dot(triu_ref[...], h2) + _dot(triu_ref[...], h3)
    lane = lax.broadcasted_iota(jnp.int32, dt.shape, 1)
    cum = jnp.where(lane < SSM_HEADS, pre, suf)
    return dt, cum


def _ssd_fwd_kernel(xp_ref, xc_ref, xn_ref, shift_ref, cw_ref, cb_ref, dt_ref, dtb_ref, a_ref, tril_ref,
                    triu_ref, e2_ref, xconv_ref, sin_ref, state_scr):
    c = pl.program_id(1)

    @pl.when(c == 0)
    def _():
        state_scr[...] = jnp.zeros_like(state_scr)

    taps_off = [kk for kk in range(CONV_WIDTH) if kk != CONV_HALO]
    for c0 in range(0, CONV_DIM, PROJ_COLS):
        cols = slice(c0, c0 + PROJ_COLS)
        ext = jnp.concatenate([xp_ref[0, :, cols], xc_ref[0, :, cols], xn_ref[0, :, cols]], axis=0)
        shifted = _dot(shift_ref[0], ext)
        acc = cb_ref[:, cols] + xc_ref[0, :, cols].astype(F32) * cw_ref[CONV_HALO:CONV_HALO + 1, cols]
        for n, kk in enumerate(taps_off):
            acc = acc + shifted[n * CHUNK:(n + 1) * CHUNK] * cw_ref[kk:kk + 1, cols]
        xconv_ref[0, :, cols] = (acc * jax.nn.sigmoid(acc)).astype(BF16)

    dt, cum = _dt_terms(dt_ref, dtb_ref, a_ref, tril_ref, triu_ref)
    total = cum[CHUNK - 1:CHUNK, :]
    fwd_lane = lax.broadcasted_iota(jnp.int32, dt.shape, 1) < SSM_HEADS
    w = jnp.exp(jnp.where(fwd_lane, total - cum, 0.0)) * dt
    wexp = _expand(w, e2_ref)
    cdec = _expand(jnp.broadcast_to(jnp.exp(total), (8, 2 * SSM_HEADS)), e2_ref)[0:1, :]
    for g in range(SSM_GROUPS):
        lo, hi = g * GROUP_WIDTH, (g + 1) * GROUP_WIDTH
        xw = (xconv_ref[0, :, lo:hi].astype(F32) * wexp[:, lo:hi]).astype(BF16)
        bg = xconv_ref[0, :, SSM_INNER + g * SSM_STATE:SSM_INNER + (g + 1) * SSM_STATE]
        st = state_scr[g]
        sin_ref[0, 0, g] = st.astype(BF16)
        state_scr[g] = st * cdec[:, lo:hi] + _dot_tn(bg, xw)


def _ssd_main_kernel(xc_ref, dt_ref, z_ref, sin_ref, attn_ref, gate_ref, xres_ref,
                     dtb_ref, a_ref, tril_ref, triu_ref, e2f_ref, e2b_ref, dskip_ref, snorm_ref,
                     wbr_ref, wout_ref, out_ref, y_scr, state_scr, exp_scr):
    c = pl.program_id(1)

    @pl.when(c == 0)
    def _():
        state_scr[...] = jnp.zeros_like(state_scr)

    dt, cum = _dt_terms(dt_ref, dtb_ref, a_ref, tril_ref, triu_ref)
    cum2 = cum * LOG2E
    pad = jnp.zeros((CHUNK, LANES - 2 * SSM_HEADS), F32)
    row_t = jnp.concatenate([cum2 - jnp.log2(dt), pad], axis=1).T
    dt_t = jnp.concatenate([dt, pad], axis=1).T
    diag_t = jnp.log2(dt_t[0:SSM_HEADS] + dt_t[SSM_HEADS:2 * SSM_HEADS])
    li = lax.broadcasted_iota(jnp.int32, (CHUNK, CHUNK), 0)
    si = lax.broadcasted_iota(jnp.int32, (CHUNK, CHUNK), 1)
    lower = li > si
    upper = li < si
    lane = lax.broadcasted_iota(jnp.int32, (CHUNK, LANES), 1)
    first_half = lane < SSM_HEAD_DIM

    first = cum[0:1, :]
    bwd_lane = lax.broadcasted_iota(jnp.int32, dt.shape, 1) >= SSM_HEADS
    ecum = jnp.exp(cum)
    exp_scr[0] = _expand(ecum, e2f_ref)
    exp_scr[1] = _expand(ecum, e2b_ref)
    exp_scr[2] = _expand(jnp.exp(jnp.where(bwd_lane, first - cum, 0.0)) * dt, e2b_ref)
    cdec = _expand(jnp.broadcast_to(jnp.exp(first), (8, 2 * SSM_HEADS)), e2b_ref)[0:1, :]

    def b_c(g):
        return (xc_ref[0, :, SSM_INNER + g * SSM_STATE:SSM_INNER + (g + 1) * SSM_STATE],
                xc_ref[0, :, SSM_INNER + SSM_GN + g * SSM_STATE:SSM_INNER + SSM_GN + (g + 1) * SSM_STATE])

    def intra(g):
        bg, cg = b_c(g)
        cb = _dot_nt(cg, bg)
        for pr in range(SSM_REP // 2):
            ms = []
            for hh in range(2):
                h = g * SSM_REP + 2 * pr + hh
                hb = SSM_HEADS + h
                arg = jnp.where(lower, cum2[:, h:h + 1] - row_t[h:h + 1, :],
                                jnp.where(upper, cum2[:, hb:hb + 1] - row_t[hb:hb + 1, :],
                                          diag_t[h:h + 1, :]))
                ms.append((cb * jnp.exp2(arg)).astype(BF16))
            col = (g * SSM_REP + 2 * pr) * SSM_HEAD_DIM
            xpair = xc_ref[0, :, col:col + LANES]
            zero = jnp.zeros_like(xpair)
            xdiag = jnp.concatenate([jnp.where(first_half, xpair, zero),
                                     jnp.where(first_half, zero, xpair)], axis=0)
            y_scr[:, col:col + LANES] = _dot(jnp.concatenate(ms, axis=1), xdiag)

    def inter(g):
        lo, hi = g * GROUP_WIDTH, (g + 1) * GROUP_WIDTH
        bg, cg = b_c(g)
        st = state_scr[g]
        xg = xc_ref[0, :, lo:hi].astype(F32)
        y = (y_scr[:, lo:hi] + _dot(cg, sin_ref[0, 0, g]) * exp_scr[0, :, lo:hi]
             + _dot(cg, st.astype(BF16)) * exp_scr[1, :, lo:hi] + dskip_ref[:, lo:hi] * xg)
        zg = z_ref[0, :, lo:hi].astype(F32)
        y_scr[:, lo:hi] = y * (zg * jax.nn.sigmoid(zg))
        xw = (xg * exp_scr[2, :, lo:hi]).astype(BF16)
        state_scr[g] = st * cdec[:, lo:hi] + _dot_tn(bg, xw)

    intra(0)
    for g in range(SSM_GROUPS):
        if g + 1 < SSM_GROUPS:
            intra(g + 1)
        inter(g)

    yn = _rms(y_scr[...], snorm_ref[...]).astype(BF16)
    y_ssm = _dot(yn, wbr_ref[...])
    merged = attn_ref[0].astype(F32) + gate_ref[0].astype(F32) * y_ssm
    out_ref[0] = xres_ref[0] + _dot(merged.astype(BF16), wout_ref[...])


def _ssd_constants():
    idx = np.arange(CHUNK)
    tril = (idx[:, None] >= idx[None, :]).astype(np.float32)
    heads = np.arange(2 * SSM_HEADS)
    chan_head = np.arange(SSM_INNER) // SSM_HEAD_DIM
    e_f = (heads[:, None] == chan_head[None, :]).astype(np.float32)
    e_b = (heads[:, None] == chan_head[None, :] + SSM_HEADS).astype(np.float32)
    stack = lambda e: jnp.asarray(np.concatenate([e, e], axis=0), BF16)
    ext_rows = CHUNK + 2 * BF16_SUBLANES
    shift = np.zeros((4, (CONV_WIDTH - 1) * CHUNK, ext_rows), np.float32)
    offs = [kk - CONV_HALO for kk in range(CONV_WIDTH) if kk != CONV_HALO]
    for n, off in enumerate(offs):
        shift[:, n * CHUNK + idx, BF16_SUBLANES + idx + off] = 1.0
    shift[0, :, :BF16_SUBLANES] = 0.0
    shift[1, :, :BF16_SUBLANES] = 0.0
    shift[0, :, BF16_SUBLANES + CHUNK:] = 0.0
    shift[2, :, BF16_SUBLANES + CHUNK:] = 0.0
    return jnp.asarray(tril, BF16), jnp.asarray(tril.T, BF16), stack(e_f), stack(e_b), jnp.asarray(shift, BF16)


def _ssd_fwd(xbc, dt_raw, conv_w, conv_b, dt_bias, a_neg, consts, b, seq_len):
    tril, triu, e2f, _, shift = consts
    nc = seq_len // CHUNK
    halo_blocks = CHUNK // BF16_SUBLANES
    n_halo = seq_len // BF16_SUBLANES
    shift_spec = pl.BlockSpec(
        (1,) + shift.shape[1:],
        lambda bi, c: (2 * (c > 0).astype(jnp.int32) + (c < nc - 1).astype(jnp.int32), 0, 0))
    return pl.pallas_call(
        _ssd_fwd_kernel,
        grid=(b, nc),
        in_specs=[pl.BlockSpec((1, BF16_SUBLANES, CONV_DIM),
                               lambda bi, c: (bi, jnp.maximum(c * halo_blocks - 1, 0), 0)),
                  pl.BlockSpec((1, CHUNK, CONV_DIM), lambda bi, c: (bi, c, 0)),
                  pl.BlockSpec((1, BF16_SUBLANES, CONV_DIM),
                               lambda bi, c: (bi, jnp.minimum((c + 1) * halo_blocks, n_halo - 1), 0)),
                  shift_spec, _const_spec(conv_w.shape), _const_spec(conv_b.shape),
                  pl.BlockSpec((1, CHUNK, 2 * SSM_HEADS), lambda bi, c: (bi, c, 0)),
                  _const_spec(dt_bias.shape), _const_spec(a_neg.shape),
                  _const_spec(tril.shape), _const_spec(triu.shape), _const_spec(e2f.shape)],
        out_specs=[pl.BlockSpec((1, CHUNK, CONV_DIM), lambda bi, c: (bi, c, 0)),
                   pl.BlockSpec((1, 1, SSM_GROUPS, SSM_STATE, GROUP_WIDTH), lambda bi, c: (bi, c, 0, 0, 0))],
        out_shape=[jax.ShapeDtypeStruct((b, seq_len, CONV_DIM), BF16),
                   jax.ShapeDtypeStruct((b, nc, SSM_GROUPS, SSM_STATE, GROUP_WIDTH), BF16)],
        scratch_shapes=[pltpu.VMEM((SSM_GROUPS, SSM_STATE, GROUP_WIDTH), F32)],
        compiler_params=_params(2),
        name="ssd_fwd",
    )(xbc, xbc, xbc, shift, conv_w, conv_b, dt_raw, dt_bias, a_neg, tril, triu, e2f)


def _ssd_main(xconv, dt_raw, z, s_in, attn_part, gate, x3d, dt_bias, a_neg, consts, d_skip_exp, ssm_norm,
              w_br_ssm, w_out, b, seq_len):
    tril, triu, e2f, e2b, _ = consts
    nc = seq_len // CHUNK
    rev =lambda width, col=0: pl.BlockSpec((1, CHUNK, width), lambda bi, c: (bi, nc - 1 - c, col))
    return pl.pallas_call(
        _ssd_main_kernel,
        grid=(b, nc),
        in_specs=[rev(CONV_DIM), rev(2 * SSM_HEADS), rev(SSM_INNER),
                  pl.BlockSpec((1, 1, SSM_GROUPS, SSM_STATE, GROUP_WIDTH),
                               lambda bi, c: (bi, nc - 1 - c, 0, 0, 0)),
                  rev(D_MODEL), rev(D_MODEL, 1), rev(D_MODEL)]
        + [_const_spec(a.shape) for a in (dt_bias, a_neg, tril, triu, e2f, e2b, d_skip_exp, ssm_norm,
                                          w_br_ssm, w_out)],
        out_specs=rev(D_MODEL),
        out_shape=jax.ShapeDtypeStruct((b, seq_len, D_MODEL), F32),
        scratch_shapes=[pltpu.VMEM((CHUNK, SSM_INNER), F32),
                        pltpu.VMEM((SSM_GROUPS, SSM_STATE, GROUP_WIDTH), F32),
                        pltpu.VMEM((3, CHUNK, SSM_INNER), F32)],
        compiler_params=_params(2),
        name="ssd_main",
    )(xconv, dt_raw, z, s_in, attn_part, gate, x3d, dt_bias, a_neg, tril, triu, e2f, e2b, d_skip_exp,
      ssm_norm, w_br_ssm, w_out)


def _memkv_kernel(mem_ref, g_ref, wkv_ref, kv_ref):
    mn = _rms(mem_ref[0], g_ref[...]).astype(BF16)
    for c0 in range(0, 2 * D_MODEL, PROJ_COLS):
        kv_ref[0, :, c0:c0 + PROJ_COLS] = _dot(mn, wkv_ref[:, c0:c0 + PROJ_COLS]).astype(BF16)


def _memkv(mem, norm_mem, w_kv):
    b, m, _ = mem.shape
    return pl.pallas_call(
        _memkv_kernel,
        grid=(b,),
        in_specs=[pl.BlockSpec((1, m, D_MODEL), lambda bi: (bi, 0, 0)),
                  _const_spec(norm_mem.shape), _const_spec(w_kv.shape)],
        out_specs=pl.BlockSpec((1, m, 2 * D_MODEL), lambda bi: (bi, 0, 0)),
        out_shape=jax.ShapeDtypeStruct((b, m, 2 * D_MODEL), BF16),
        compiler_params=_params(1),
        name="mem_kv",
    )(mem, norm_mem, w_kv)


def _tail_kernel(x_ref, kv_ref, ncross_ref, wq_ref, wo_ref, nffn_ref, wg_ref, wu_ref, wd_ref, nfin_ref,
                 out_ref, o_scr, h_scr):
    x = x_ref[0]
    u = _rms(x, ncross_ref[...]).astype(BF16)
    scale = X_HEAD_DIM ** -0.5
    for h in range(X_HEADS):
        lo, hi = h * X_HEAD_DIM, (h + 1) * X_HEAD_DIM
        q = _dot(u, wq_ref[:, lo:hi]).astype(BF16)
        s = _dot_nt(q, kv_ref[0, :, lo:hi]) * scale
        p = jnp.exp(s - jnp.max(s, axis=-1, keepdims=True))
        denom = jnp.sum(p, axis=-1, keepdims=True)
        o = _dot(p.astype(BF16), kv_ref[0, :, D_MODEL + lo:D_MODEL + hi]) / denom
        o_scr[:, lo:hi] = o.astype(BF16)
    x = x + _dot(o_scr[...], wo_ref[...])
    u = _rms(x, nffn_ref[...]).astype(BF16)
    for c0 in range(0, FFN_HIDDEN, PROJ_COLS):
        c1 = min(c0 + PROJ_COLS, FFN_HIDDEN)
        gte = _dot(u, wg_ref[:, c0:c1])
        up = _dot(u, wu_ref[:, c0:c1])
        h_scr[:, c0:c1] = (gte * jax.nn.sigmoid(gte) * up).astype(BF16)
    x = x + _dot(h_scr[...], wd_ref[...])
    out_ref[0] = _rms(x, nfin_ref[...])


def _tail(x3d, kv, norm_cross, w_q, w_o, norm_ffn, w_gate, w_up, w_down, norm_final):
    b, seq_len, _ = x3d.shape
    m = kv.shape[1]
    consts = (norm_cross, w_q, w_o, norm_ffn, w_gate, w_up, w_down, norm_final)
    return pl.pallas_call(
        _tail_kernel,
        grid=(b, seq_len // ROW_TILE),
        in_specs=[pl.BlockSpec((1, ROW_TILE, D_MODEL), lambda bi, i: (bi, i, 0)),
                  pl.BlockSpec((1, m, 2 * D_MODEL), lambda bi, i: (bi, 0, 0))]
        + [_const_spec(a.shape) for a in consts],
        out_specs=pl.BlockSpec((1, ROW_TILE, D_MODEL), lambda bi, i: (bi, i, 0)),
        out_shape=jax.ShapeDtypeStruct((b, seq_len, D_MODEL), F32),
        scratch_shapes=[pltpu.VMEM((ROW_TILE, D_MODEL), BF16), pltpu.VMEM((ROW_TILE, FFN_HIDDEN), BF16)],
        compiler_params=_params(2),
        name="cross_ffn",
    )(x3d, kv, *consts)


def _prepare(norm_mix, w_in, conv_w, conv_b, attn_sink, a_log, dt_bias, d_skip, ssm_norm, w_br_attn,
             w_br_ssm, w_out, norm_cross, norm_mem, w_q_cross, w_kv_cross, w_o_cross, norm_ffn, w_gate_up,
             w_down, norm_final):
    row = lambda v: v.reshape(1, -1).astype(F32)
    w = w_in[0]
    cuts = np.cumsum([0, ATT_WIDTH + 2 * KV_WIDTH, SSM_INNER, CONV_DIM, 2 * SSM_HEADS, 2 * D_MODEL])
    wqkv, wz, wxbc, wdt, wgate = (w[:, int(s):int(e)].astype(BF16) for s, e in zip(cuts[:-1], cuts[1:]))
    order = np.asarray(_ATT_HEAD_ORDER)
    head_cols = (order[:, None] * HEAD_DIM + np.arange(HEAD_DIM)[None, :]).reshape(-1)
    wqkv = jnp.concatenate([wqkv[:, head_cols], wqkv[:, ATT_WIDTH:]], axis=1)
    slopes = jnp.exp2(-8.0 * jnp.arange(1, ATT_HEADS + 1, dtype=F32) / ATT_HEADS)
    return dict(
        norm_mix=row(norm_mix[0]), wqkv=wqkv, wz=wz, wxbc=wxbc, wdt=wdt, wgate=wgate,
        conv_w=conv_w[0].astype(F32), conv_b=row(conv_b[0]),
        slopes=slopes[order] * LOG2E, sink=attn_sink[0].astype(F32)[order] * LOG2E,
        a_neg=row(-jnp.exp(a_log[0].astype(F32))), dt_bias=row(dt_bias[0]),
        d_skip=row(jnp.repeat(d_skip[0].astype(F32), SSM_HEAD_DIM)), ssm_norm=row(ssm_norm[0]),
        w_br_attn=w_br_attn[0][head_cols, :].astype(BF16), w_br_ssm=w_br_ssm[0].astype(BF16),
        w_out=w_out[0].astype(BF16),
        norm_cross=row(norm_cross[0]), norm_mem=row(norm_mem[0]),
        w_q=w_q_cross[0].astype(BF16), w_kv=w_kv_cross[0].astype(BF16), w_o=w_o_cross[0].astype(BF16),
        norm_ffn=row(norm_ffn[0]), w_gate=w_gate_up[0][:, :FFN_HIDDEN].astype(BF16),
        w_up=w_gate_up[0][:, FFN_HIDDEN:].astype(BF16), w_down=w_down[0].astype(BF16),
        norm_final=row(norm_final), consts=_ssd_constants())


def _trunk(x, mem, p):
    b, seq_len, _ = x.shape
    assert seq_len % ROW_TILE == 0 and seq_len % ATT_TQ == 0 and seq_len % CHUNK == 0
    qkv, z, xbc, dt_raw, gate = _inproj(x.reshape(b * seq_len, D_MODEL), p["norm_mix"], p["wqkv"], p["wz"],
                                        p["wxbc"], p["wdt"], p["wgate"])
    as3d = lambda t: t.reshape(b, seq_len, t.shape[-1])
    qkv, z, xbc, dt_raw, gate = as3d(qkv), as3d(z), as3d(xbc), as3d(dt_raw), as3d(gate)
    attn_part = _attention(qkv, gate, p["slopes"], p["sink"], p["w_br_attn"], b, seq_len)
    xconv, s_in = _ssd_fwd(xbc, dt_raw, p["conv_w"], p["conv_b"], p["dt_bias"], p["a_neg"], p["consts"],
                           b, seq_len)
    x1 = _ssd_main(xconv, dt_raw, z, s_in, attn_part, gate, x, p["dt_bias"], p["a_neg"], p["consts"],
                   p["d_skip"], p["ssm_norm"], p["w_br_ssm"], p["w_out"], b, seq_len)
    kv = _memkv(mem, p["norm_mem"], p["w_kv"])
    return _tail(x1, kv, p["norm_cross"], p["w_q"], p["w_o"], p["norm_ffn"], p["w_gate"], p["w_up"],
                 p["w_down"], p["norm_final"])


def kernel(x_prompt, x_sample, mem_prompt, mem_sample, norm_mix, w_in, conv_w, conv_b, attn_sink, a_log,
           dt_bias, d_skip, ssm_norm, w_br_attn, w_br_ssm, w_out, norm_cross, norm_mem, w_q_cross,
           w_kv_cross, w_o_cross, norm_ffn, w_gate_up, w_down, norm_final):
    p = _prepare(norm_mix, w_in, conv_w, conv_b, attn_sink, a_log, dt_bias, d_skip, ssm_norm, w_br_attn,
                 w_br_ssm, w_out, norm_cross, norm_mem, w_q_cross, w_kv_cross, w_o_cross, norm_ffn,
                 w_gate_up, w_down, norm_final)
    return (_trunk(x_prompt, mem_prompt, p), _trunk(x_sample, mem_sample, p))
```

```python
import functools

import numpy as np
import jax
import jax.numpy as jnp
from jax import lax
from jax.experimental import pallas as pl
from jax.experimental.pallas import tpu as pltpu

F32 = jnp.float32
BF16 = jnp.bfloat16

D_MODEL = 1024
EPS = 1e-6
ATT_HEADS = 16
ATT_KV_HEADS = 4
ATT_REP = ATT_HEADS // ATT_KV_HEADS
HEAD_DIM = 64
ATT_WIDTH = ATT_HEADS * HEAD_DIM
KV_WIDTH = ATT_KV_HEADS * HEAD_DIM
WINDOW = 128
SSM_INNER = 2 * D_MODEL
SSM_HEAD_DIM = 64
SSM_HEADS = SSM_INNER // SSM_HEAD_DIM
SSM_GROUPS = 4
SSM_REP = SSM_HEADS // SSM_GROUPS
SSM_STATE = 128
SSM_GN = SSM_GROUPS * SSM_STATE
CONV_WIDTH = 5
CONV_DIM = SSM_INNER + 2 * SSM_GN
CHUNK = 128
GROUP_WIDTH = SSM_REP * SSM_HEAD_DIM
X_HEADS = 4
X_HEAD_DIM = D_MODEL // X_HEADS
FFN_HIDDEN = -(-8 * D_MODEL // (3 * 256)) * 256

LANES = 128
BF16_SUBLANES = 16
VMEM_LIMIT_BYTES = 56 * 1024 * 1024

ROW_TILE = 512
ATT_TQ = 256
ATT_SKEW = 4
PROJ_COLS = 512
CONV_COLS = 512

LOG2E = 1.4426950408889634
Q_SCALE = HEAD_DIM ** -0.5 * LOG2E
XQ_SCALE = X_HEAD_DIM ** -0.5 * LOG2E


def _rms(x, g):
    return x * lax.rsqrt(jnp.mean(x * x, axis=-1, keepdims=True) + EPS) * g


def _const_spec(shape):
    nd = len(shape)
    return pl.BlockSpec(shape, lambda *_: (0,) * nd, pipeline_mode=pl.Buffered(1))


def _params(n_grid):
    return pltpu.CompilerParams(dimension_semantics=("arbitrary",) * n_grid,
                                vmem_limit_bytes=VMEM_LIMIT_BYTES)


def _dot(a, b):
    return jnp.dot(a, b, preferred_element_type=F32)


def _dot_nt(a, b):
    return lax.dot_general(a, b, (((1,), (1,)), ((), ())), preferred_element_type=F32)


def _dot_tn(a, b):
    return lax.dot_general(a, b, (((0,), (0,)), ((), ())), preferred_element_type=F32)


def _split3(v):
    h1 = v.astype(BF16)
    r1 = v - h1.astype(F32)
    h2 = r1.astype(BF16)
    r2 = r1 - h2.astype(F32)
    return h1, h2, r2.astype(BF16)


def _exact_dot(sel, v):
    h1, h2, h3 = _split3(v)
    return _dot(sel, h1) + _dot(sel, h2) + _dot(sel, h3)


def _expand(v, e2_ref):
    hi = v.astype(BF16)
    lo = (v - hi.astype(F32)).astype(BF16)
    return _dot(jnp.concatenate([hi, lo], axis=1), e2_ref[...])


def _inproj_kernel(x_ref, g_ref, wqkv_ref, wz_ref, wxbc_ref, wdt_ref, wgate_ref,
                   qkv_ref, z_ref, xbc_ref, dt_ref, gate_ref):
    u = _rms(x_ref[...], g_ref[...]).astype(BF16)

    def proj(w_ref, o_ref, post):
        n = w_ref.shape[1]
        for c0 in range(0, n, PROJ_COLS):
            c1 = min(c0 + PROJ_COLS, n)
            o_ref[:, c0:c1] = post(_dot(u, w_ref[:, c0:c1])).astype(o_ref.dtype)

    ident = lambda t: t
    for c0 in range(0, ATT_WIDTH, PROJ_COLS):
        qkv_ref[:, c0:c0 + PROJ_COLS] = (_dot(u, wqkv_ref[:, c0:c0 + PROJ_COLS]) * Q_SCALE).astype(BF16)
    qkv_ref[:, ATT_WIDTH:] = _dot(u, wqkv_ref[:, ATT_WIDTH:]).astype(BF16)
    proj(wz_ref, z_ref, ident)
    proj(wxbc_ref, xbc_ref, ident)
    proj(wdt_ref, dt_ref, ident)
    proj(wgate_ref, gate_ref, jax.nn.sigmoid)


def _inproj(x2d, norm_mix, wqkv, wz, wxbc, wdt, wgate):
    t = x2d.shape[0]
    row = lambda n: pl.BlockSpec((ROW_TILE, n), lambda i: (i, 0))
    widths = (wqkv.shape[1], wz.shape[1], wxbc.shape[1], wdt.shape[1], wgate.shape[1])
    dtypes = (BF16, BF16, BF16, F32, BF16)
    return pl.pallas_call(
        _inproj_kernel,
        grid=(t // ROW_TILE,),
        in_specs=[row(D_MODEL), _const_spec((1, D_MODEL))]
        + [_const_spec(w.shape) for w in (wqkv, wz, wxbc, wdt, wgate)],
        out_specs=[row(n) for n in widths],
        out_shape=[jax.ShapeDtypeStruct((t, n), dt) for n, dt in zip(widths, dtypes)],
        compiler_params=_params(1),
        name="inproj",
    )(x2d, norm_mix, wqkv, wz, wxbc, wdt, wgate)


def _attn_kernel(seq_len, q_ref, kp_ref, kc_ref, kn_ref, vp_ref, vc_ref, vn_ref, half_ref, slope_ref,
                 sink_ref, wbr_ref, gate_ref, out_ref, ot_scr, s_scr):
    i = pl.program_id(1)
    tk = ATT_TQ + 2 * WINDOW
    j = lax.broadcasted_iota(jnp.int32, (tk, ATT_TQ), 0)
    r = lax.broadcasted_iota(jnp.int32, (tk, ATT_TQ), 1)
    dist = jnp.abs(r + WINDOW - j)
    kpos = i * ATT_TQ - WINDOW + j
    valid = (dist <= WINDOW) & (kpos >= 0) & (kpos < seq_len)
    mdist = jnp.where(valid, dist.astype(F32), jnp.inf)
    k = jnp.concatenate([kp_ref[0], kc_ref[0], kn_ref[0]], axis=0)
    v = jnp.concatenate([vp_ref[0], vc_ref[0], vn_ref[0]], axis=0)
    k_halves, v2t = [], []
    for pair in range(ATT_KV_HEADS // 2):
        k2 = k[:, pair * LANES:(pair + 1) * LANES]
        k_halves.append((k2 * half_ref[0:1, :], k2 * half_ref[1:2, :]))
        v2t.append(v[:, pair * LANES:(pair + 1) * LANES].T)

    def logits(pos):
        tile, half = pos // 2, pos % 2
        qt = q_ref[0, :, tile * LANES:(tile + 1) * LANES]
        s_scr[pos] = _dot_nt(k_halves[tile // ATT_REP][half], qt) - slope_ref[pos] * mdist

    def attend(pos):
        tile, half = pos // 2, pos % 2
        s = s_scr[pos]
        sink = sink_ref[pos]
        m = jnp.maximum(jnp.max(s, axis=0, keepdims=True), sink)
        p = jnp.exp2(s - m)
        denom = jnp.sum(p, axis=0, keepdims=True) + jnp.exp2(sink - m)
        vt = v2t[tile // ATT_REP][half * HEAD_DIM:(half + 1) * HEAD_DIM]
        ot_scr[pos * HEAD_DIM:(pos + 1) * HEAD_DIM, :] = (_dot(vt, p.astype(BF16)) / denom).astype(BF16)

    for pos in range(ATT_HEADS + ATT_SKEW):
        if pos < ATT_HEADS:
            logits(pos)
        if pos >= ATT_SKEW:
            attend(pos - ATT_SKEW)
    y = _dot_tn(ot_scr[...], wbr_ref[...])
    out_ref[0] = (gate_ref[0].astype(F32) * y).astype(out_ref.dtype)


_ATT_HEAD_ORDER = [ATT_REP * (2 * (p // (2 * ATT_REP)) + p % 2) + (p // 2) % ATT_REP for p in range(ATT_HEADS)]


def _attention(qkv, gate, slopes, sink, w_br_attn, b, seq_len):
    nq = seq_len // ATT_TQ
    halo_per_q = ATT_TQ // WINDOW
    n_halo = seq_len // WINDOW
    k_col, v_col = ATT_WIDTH // KV_WIDTH, ATT_WIDTH // KV_WIDTH + 1
    prev = lambda col: pl.BlockSpec(
        (1, WINDOW, KV_WIDTH), lambda bi, i: (bi, jnp.maximum(i * halo_per_q - 1, 0), col))
    cur = lambda col: pl.BlockSpec((1, ATT_TQ, KV_WIDTH), lambda bi, i: (bi, i, col))
    nxt = lambda col: pl.BlockSpec(
        (1, WINDOW, KV_WIDTH), lambda bi, i: (bi, jnp.minimum((i + 1) * halo_per_q, n_halo - 1), col))
    smem = pl.BlockSpec(memory_space=pltpu.SMEM)
    lane_half = np.arange(LANES) // HEAD_DIM
    half_masks = jnp.asarray(np.stack([lane_half == 0, lane_half == 1]), BF16)
    return pl.pallas_call(
        functools.partial(_attn_kernel, seq_len),
        grid=(b, nq),
        in_specs=[pl.BlockSpec((1, ATT_TQ, ATT_WIDTH), lambda bi, i: (bi, i, 0)),
                  prev(k_col), cur(k_col), nxt(k_col), prev(v_col), cur(v_col), nxt(v_col),
                  _const_spec(half_masks.shape), smem, smem, _const_spec(w_br_attn.shape),
                  pl.BlockSpec((1, ATT_TQ, D_MODEL), lambda bi, i: (bi, i, 0))],
        out_specs=pl.BlockSpec((1, ATT_TQ, D_MODEL), lambda bi, i: (bi, i, 0)),
        out_shape=jax.ShapeDtypeStruct((b, seq_len, D_MODEL), BF16),
        scratch_shapes=[pltpu.VMEM((ATT_WIDTH, ATT_TQ), BF16),
                        pltpu.VMEM((ATT_HEADS, ATT_TQ + 2 * WINDOW, ATT_TQ), F32)],
        compiler_params=_params(2),
        name="window_attn",
    )(qkv, qkv, qkv, qkv, qkv, qkv, qkv, half_masks, slopes, sink, w_br_attn, gate)


CONV_HALO = CONV_WIDTH // 2


def _dt_terms(dt_ref, dtb_ref, a_ref, tril_ref, triu_ref):
    dt = jax.nn.softplus(dt_ref[0] + dtb_ref[...])
    dta = dt * a_ref[...]
    h1, h2, h3 = _split3(dta)
    pre = _dot(tril_ref[...], h1) + _dot(tril_ref[...], h2) + _dot(tril_ref[...], h3)
    suf = _dot(triu_ref[...], h1) + _dot(triu_ref[...], h2) + _dot(triu_ref[...], h3)
    lane = lax.broadcasted_iota(jnp.int32, dt.shape, 1)
    cum = jnp.where(lane < SSM_HEADS, pre, suf)
    return dt, cum


def _ssd_fwd_kernel(xp_ref, xc_ref, xn_ref, shift_ref, cw_ref, cb_ref, dt_ref, dtb_ref, a_ref, tril_ref,
                    triu_ref, e2_ref, xconv_ref, sin_ref, tcol_ref, trow_ref, state_scr):
    c = pl.program_id(1)

    @pl.when(c == 0)
    def _():
        state_scr[...] = jnp.zeros_like(state_scr)

    taps_off = [kk for kk in range(CONV_WIDTH) if kk != CONV_HALO]
    for c0 in range(0, CONV_DIM, CONV_COLS):
        cols = slice(c0, c0 + CONV_COLS)
        ext = jnp.concatenate([xp_ref[0, :, cols], xc_ref[0, :, cols], xn_ref[0, :, cols]], axis=0)
        taps = jnp.concatenate([ext * cw_ref[kk:kk + 1, cols].astype(BF16) for kk in taps_off], axis=0)
        acc = (_dot(shift_ref[0], taps) + cb_ref[:, cols]
               + xc_ref[0, :, cols].astype(F32) * cw_ref[CONV_HALO:CONV_HALO + 1, cols])
        xconv_ref[0, :, cols] = (acc * jax.nn.sigmoid(acc)).astype(BF16)

    dt, cum = _dt_terms(dt_ref, dtb_ref, a_ref, tril_ref, triu_ref)
    total = cum[CHUNK - 1:CHUNK, :]
    first = cum[0:1, :]
    lane64 = lax.broadcasted_iota(jnp.int32, dt.shape, 1)
    fwd_lane = lane64 < SSM_HEADS

    pad = jnp.zeros((CHUNK, LANES - 2 * SSM_HEADS), F32)
    wide = lambda t: jnp.concatenate([t, pad[:t.shape[0]]], axis=1)
    cum2 = cum * LOG2E
    tcol_ref[0, 0, 0:CHUNK] = wide(cum2)
    tcol_ref[0, 0, CHUNK:2 * CHUNK] = wide(jnp.exp(cum))
    tcol_ref[0, 0, 2 * CHUNK:3 * CHUNK] = wide(jnp.exp(jnp.where(fwd_lane, 0.0, first - cum)) * dt)
    tcol_ref[0, 0, 3 * CHUNK:] = wide(jnp.broadcast_to(jnp.exp(first), (8, 2 * SSM_HEADS)))
    row_t = wide(cum2 - jnp.log2(dt)).T
    dt_t = wide(dt).T
    diag_t = jnp.log2(dt_t[0:SSM_HEADS] + dt_t[SSM_HEADS:2 * SSM_HEADS])
    full = jnp.concatenate([row_t[0:2 * SSM_HEADS], diag_t,
                            jnp.zeros((LANES - 3 * SSM_HEADS, CHUNK), F32)], axis=0)
    swapped = pltpu.roll(full, CHUNK // 2, axis=1)
    next_row = lambda t: pltpu.roll(t, LANES - 1, axis=0)
    low = lax.broadcasted_iota(jnp.int32, full.shape, 1) < CHUNK // 2
    trow_ref[0, 0, 0] = jnp.where(low, full, next_row(swapped))
    trow_ref[0, 0, 1] = jnp.where(low, swapped, next_row(full))

    w = jnp.exp(jnp.where(fwd_lane, total - cum, 0.0)) * dt
    wexp = _expand(w, e2_ref)
    cdec = _expand(jnp.broadcast_to(jnp.exp(total), (8, 2 * SSM_HEADS)), e2_ref)[0:1, :]
    for g in range(SSM_GROUPS):
        lo, hi = g * GROUP_WIDTH, (g + 1) * GROUP_WIDTH
        xw = (xconv_ref[0, :, lo:hi].astype(F32) * wexp[:, lo:hi]).astype(BF16)
        bg = xconv_ref[0, :, SSM_INNER + g * SSM_STATE:SSM_INNER + (g + 1) * SSM_STATE]
        st = state_scr[g]
        sin_ref[0, 0, g] = st.astype(BF16)
        state_scr[g] = st * cdec[:, lo:hi] + _dot_tn(bg, xw)


def _ssd_main_kernel(xc_ref, tcol_ref, trow_ref, z_ref, sin_ref, attn_ref, gate_ref, xres_ref,
                     half_ref, e2f_ref, e2b_ref, dskip_ref, snorm_ref,
                     wbr_ref, wout_ref, out_ref, y_scr, state_scr, exp_scr, yn_scr, cb_scr, m_scr):
    c = pl.program_id(1)

    @pl.when(c == 0)
    def _():
        state_scr[...] = jnp.zeros_like(state_scr)
        yn_scr[...] = jnp.zeros_like(yn_scr)

    half = CHUNK // 2
    low = lax.broadcasted_iota(jnp.int32, (CHUNK, LANES), 1) < half

    def b_c(g):
        return (xc_ref[0, :, SSM_INNER + g * SSM_STATE:SSM_INNER + (g + 1) * SSM_STATE],
                xc_ref[0, :, SSM_INNER + SSM_GN + g * SSM_STATE:SSM_INNER + SSM_GN + (g + 1) * SSM_STATE])

    for g in range(SSM_GROUPS):
        bg, cg = b_c(g)
        cb = _dot_nt(cg, bg)
        cb_sw = pltpu.roll(cb, half, axis=1)
        cb_scr[g, 0] = jnp.where(low, cb, cb_sw)
        cb_scr[g, 1] = jnp.where(low, cb_sw, cb)

    y_ssm = _dot(yn_scr[...], wbr_ref[...])
    merged = attn_ref[0].astype(F32) + gate_ref[0].astype(F32) * y_ssm
    out_ref[0] = xres_ref[0] + _dot(merged.astype(BF16), wout_ref[...])

    exp_scr[0] = _expand(tcol_ref[0, 0, CHUNK:2 * CHUNK], e2f_ref)
    exp_scr[1] = _expand(tcol_ref[0, 0, CHUNK:2 * CHUNK], e2b_ref)
    exp_scr[2] = _expand(tcol_ref[0, 0, 2 * CHUNK:3 * CHUNK], e2b_ref)
    cdec = _expand(tcol_ref[0, 0, 3 * CHUNK:], e2b_ref)[0:1, :]

    cum2 = tcol_ref[0, 0, 0:CHUNK]
    src = lax.broadcasted_iota(jnp.int32, (half, LANES), 1) % half
    row = lax.broadcasted_iota(jnp.int32, (half, LANES), 0)
    lower, upper = row > src, row < src
    top, bot = slice(0, half), slice(half, CHUNK)

    def decay_matrices(g):
        for pr in range(SSM_REP // 2):
            h = g * SSM_REP + 2 * pr
            hb = SSM_HEADS + h
            col_f = jnp.take_along_axis(cum2, jnp.where(low, h, h + 1), axis=1)
            col_b = jnp.take_along_axis(cum2, jnp.where(low, hb, hb + 1), axis=1)
            row_f = [trow_ref[0, 0, j, h:h + 1, :] for j in range(2)]
            row_b = [trow_ref[0, 0, j, hb:hb + 1, :] for j in range(2)]
            diag = [trow_ref[0, 0, j, 2 * SSM_HEADS + h:2 * SSM_HEADS + h + 1, :] for j in range(2)]
            mixed = lambda rows, j: jnp.where(lower, col_f[rows] - row_f[j],
                                              jnp.where(upper, col_b[rows] - row_b[j], diag[j]))
            args = [[mixed(top, 0), col_b[top] - row_b[1]],
                    [col_f[bot] - row_f[0], mixed(bot, 1)]]
            for j in range(2):
                for i, rows in enumerate((top, bot)):
                    m_scr[h // 2, rows, j * LANES:(j + 1) * LANES] = (
                        cb_scr[g, j, rows, :] * jnp.exp2(args[i][j])).astype(BF16)

    def inter(g):
        lo, hi = g * GROUP_WIDTH, (g + 1) * GROUP_WIDTH
        bg, cg = b_c(g)
        m0, m1 = half_ref[0:1, :], half_ref[1:2, :]
        for pr in range(SSM_REP // 2):
            h = g * SSM_REP + 2 * pr
            col = h * SSM_HEAD_DIM
            xa = xc_ref[0, 0:half, col:col + LANES]
            xb = xc_ref[0, half:, col:col + LANES]
            xdiag = jnp.concatenate([xa * m0, xa * m1, xb * m0, xb * m1], axis=0)
            y_scr[:, col:col + LANES] = _dot(m_scr[h // 2], xdiag)
        st = state_scr[g]
        xg = xc_ref[0, :, lo:hi].astype(F32)
        y = (y_scr[:, lo:hi] + _dot(cg, sin_ref[0, 0, g]) * exp_scr[0, :, lo:hi]
             + _dot(cg, st.astype(BF16)) * exp_scr[1, :, lo:hi] + dskip_ref[:, lo:hi] * xg)
        zg = z_ref[0, :, lo:hi].astype(F32)
        y_scr[:, lo:hi] = y * (zg * jax.nn.sigmoid(zg))
        xw = (xg * exp_scr[2, :, lo:hi]).astype(BF16)
        state_scr[g] = st * cdec[:, lo:hi] + _dot_tn(bg, xw)

    for g in range(SSM_GROUPS):
        decay_matrices(g)
    for g in range(SSM_GROUPS):
        inter(g)

    yn_scr[...] = _rms(y_scr[...], snorm_ref[...]).astype(BF16)


def _ssd_constants():
    idx = np.arange(CHUNK)
    tril = (idx[:, None] >= idx[None, :]).astype(np.float32)
    heads = np.arange(2 * SSM_HEADS)
    chan_head = np.arange(SSM_INNER) // SSM_HEAD_DIM
    e_f = (heads[:, None] == chan_head[None, :]).astype(np.float32)
    e_b = (heads[:, None] == chan_head[None, :] + SSM_HEADS).astype(np.float32)
    stack = lambda e: jnp.asarray(np.concatenate([e, e], axis=0), BF16)
    widen = lambda e: np.concatenate([e, np.zeros((LANES - e.shape[0], e.shape[1]), np.float32)], axis=0)
    ext_rows = CHUNK + 2 * BF16_SUBLANES
    taps_off = [kk for kk in range(CONV_WIDTH) if kk != CONV_HALO]
    shift = np.zeros((4, CHUNK, len(taps_off) * ext_rows), np.float32)
    for n, kk in enumerate(taps_off):
        shift[:, idx, n * ext_rows + BF16_SUBLANES + idx + kk - CONV_HALO] = 1.0
        lo_halo = slice(n * ext_rows, n * ext_rows + BF16_SUBLANES)
        hi_halo = slice(n * ext_rows + BF16_SUBLANES + CHUNK, (n + 1) * ext_rows)
        shift[0, :, lo_halo] = 0.0
        shift[1, :, lo_halo] = 0.0
        shift[0, :, hi_halo] = 0.0
        shift[2, :, hi_halo] = 0.0
    lane_half = np.arange(LANES) // SSM_HEAD_DIM
    return dict(tril=jnp.asarray(tril, BF16), triu=jnp.asarray(tril.T, BF16), e2f=stack(e_f),
                e2f_wide=stack(widen(e_f)), e2b_wide=stack(widen(e_b)), shift=jnp.asarray(shift, BF16),
                half_masks=jnp.asarray(np.stack([lane_half == 0, lane_half == 1]), BF16))


TCOL_ROWS = 3 * CHUNK + 8


def _ssd_fwd(xbc, dt_raw, conv_w, conv_b, dt_bias, a_neg, consts, b, seq_len):
    tril, triu, e2f, shift = consts["tril"], consts["triu"], consts["e2f"], consts["shift"]
    nc = seq_len // CHUNK
    halo_blocks = CHUNK // BF16_SUBLANES
    n_halo = seq_len // BF16_SUBLANES
    shift_spec = pl.BlockSpec(
        (1,) + shift.shape[1:],
        lambda bi, c: (2 * (c > 0).astype(jnp.int32) + (c < nc - 1).astype(jnp.int32), 0, 0))
    return pl.pallas_call(
        _ssd_fwd_kernel,
        grid=(b, nc),
        in_specs=[pl.BlockSpec((1, BF16_SUBLANES, CONV_DIM),
                               lambda bi, c: (bi, jnp.maximum(c * halo_blocks - 1, 0), 0)),
                  pl.BlockSpec((1, CHUNK, CONV_DIM), lambda bi, c: (bi, c, 0)),
                  pl.BlockSpec((1, BF16_SUBLANES, CONV_DIM),
                               lambda bi, c: (bi, jnp.minimum((c + 1) * halo_blocks, n_halo - 1), 0)),
                  shift_spec, _const_spec(conv_w.shape), _const_spec(conv_b.shape),
                  pl.BlockSpec((1, CHUNK, 2 * SSM_HEADS), lambda bi, c: (bi, c, 0)),
                  _const_spec(dt_bias.shape), _const_spec(a_neg.shape),
                  _const_spec(tril.shape), _const_spec(triu.shape), _const_spec(e2f.shape)],
        out_specs=[pl.BlockSpec((1, CHUNK, CONV_DIM), lambda bi, c: (bi, c, 0)),
                   pl.BlockSpec((1, 1, SSM_GROUPS, SSM_STATE, GROUP_WIDTH), lambda bi, c: (bi, c, 0, 0, 0)),
                   pl.BlockSpec((1, 1, TCOL_ROWS, LANES), lambda bi, c: (bi, c, 0, 0)),
                   pl.BlockSpec((1, 1, 2, LANES, CHUNK), lambda bi, c: (bi, c, 0, 0, 0))],
        out_shape=[jax.ShapeDtypeStruct((b, seq_len, CONV_DIM), BF16),
                   jax.ShapeDtypeStruct((b, nc, SSM_GROUPS, SSM_STATE, GROUP_WIDTH), BF16),
                   jax.ShapeDtypeStruct((b, nc, TCOL_ROWS, LANES), F32),
                   jax.ShapeDtypeStruct((b, nc, 2, LANES, CHUNK), F32)],
        scratch_shapes=[pltpu.VMEM((SSM_GROUPS, SSM_STATE, GROUP_WIDTH), F32)],
        compiler_params=_params(2),
        name="ssd_fwd",
    )(xbc, xbc, xbc, shift, conv_w, conv_b, dt_raw, dt_bias, a_neg, tril, triu, e2f)


def _ssd_main(xconv, tcol, trow, z, s_in, attn_part, gate, x3d, consts, d_skip_exp, ssm_norm, w_br_ssm, w_out,
              b, seq_len):
    nc = seq_len // CHUNK
    ssd_chunk = lambda c: nc - 1 - jnp.minimum(c, nc - 1)
    proj_chunk = lambda c: nc - 1 - jnp.maximum(c - 1, 0)
    rev = lambda width: pl.BlockSpec((1, CHUNK, width), lambda bi, c: (bi, ssd_chunk(c), 0))
    rev_chunk = lambda shape: pl.BlockSpec((1, 1) + shape, lambda bi, c: (bi, ssd_chunk(c)) + (0,) * len(shape))
    late = lambda col=0: pl.BlockSpec((1, CHUNK, D_MODEL), lambda bi, c: (bi, proj_chunk(c), col))
    small = (consts["half_masks"], consts["e2f_wide"], consts["e2b_wide"], d_skip_exp, ssm_norm, w_br_ssm, w_out)
    return pl.pallas_call(
        _ssd_main_kernel,
        grid=(b, nc + 1),
        in_specs=[rev(CONV_DIM), rev_chunk((TCOL_ROWS, LANES)), rev_chunk((2, LANES, CHUNK)), rev(SSM_INNER),
                  rev_chunk((SSM_GROUPS, SSM_STATE, GROUP_WIDTH)), late(), late(1), late()]
        + [_const_spec(a.shape) for a in small],
        out_specs=late(),
        out_shape=jax.ShapeDtypeStruct((b, seq_len, D_MODEL), F32),
        scratch_shapes=[pltpu.VMEM((CHUNK, SSM_INNER), F32),
                        pltpu.VMEM((SSM_GROUPS, SSM_STATE, GROUP_WIDTH), F32),
                        pltpu.VMEM((3, CHUNK, SSM_INNER), F32),
                        pltpu.VMEM((CHUNK, SSM_INNER), BF16),
                        pltpu.VMEM((SSM_GROUPS, 2, CHUNK, LANES), F32),
                        pltpu.VMEM((SSM_HEADS // 2, CHUNK, 2 * LANES), BF16)],
        compiler_params=_params(2),
        name="ssd_main",
    )(xconv, tcol, trow, z, s_in, attn_part, gate, x3d, *small)


def _memkv_kernel(mem_ref, g_ref, wkv_ref, kv_ref):
    mn = _rms(mem_ref[0], g_ref[...]).astype(BF16)
    for c0 in range(0, 2 * D_MODEL, PROJ_COLS):
        kv_ref[0, :, c0:c0 + PROJ_COLS] = _dot(mn, wkv_ref[:, c0:c0 + PROJ_COLS]).astype(BF16)


def _memkv(mem, norm_mem, w_kv):
    b, m, _ = mem.shape
    return pl.pallas_call(
        _memkv_kernel,
        grid=(b,),
        in_specs=[pl.BlockSpec((1, m, D_MODEL), lambda bi: (bi, 0, 0)),
                  _const_spec(norm_mem.shape), _const_spec(w_kv.shape)],
        out_specs=pl.BlockSpec((1, m, 2 * D_MODEL), lambda bi: (bi, 0, 0)),
        out_shape=jax.ShapeDtypeStruct((b, m, 2 * D_MODEL), BF16),
        compiler_params=_params(1),
        name="mem_kv",
    )(mem, norm_mem, w_kv)


def _tail_kernel(x_ref, kv_ref, ncross_ref, wq_ref, wo_ref, nffn_ref, wg_ref, wu_ref, wd_ref, nfin_ref,
                 out_ref, o_scr, h_scr, q_scr, s_scr):
    x = x_ref[0]
    u = _rms(x, ncross_ref[...]).astype(BF16)
    for c0 in range(0, D_MODEL, PROJ_COLS):
        q_scr[:, c0:c0 + PROJ_COLS] = (_dot(u, wq_ref[:, c0:c0 + PROJ_COLS]) * XQ_SCALE).astype(BF16)

    def logits(h):
        lo, hi = h * X_HEAD_DIM, (h + 1) * X_HEAD_DIM
        s_scr[h] = _dot_nt(q_scr[:, lo:hi], kv_ref[0, :, lo:hi])

    def attend(h):
        lo, hi = h * X_HEAD_DIM, (h + 1) * X_HEAD_DIM
        s = s_scr[h]
        p = jnp.exp2(s - jnp.max(s, axis=-1, keepdims=True))
        denom = jnp.sum(p, axis=-1, keepdims=True)
        o = _dot(p.astype(BF16), kv_ref[0, :, D_MODEL + lo:D_MODEL + hi]) / denom
        o_scr[:, lo:hi] = o.astype(BF16)

    logits(0)
    for h in range(X_HEADS):
        if h + 1 < X_HEADS:
            logits(h + 1)
        attend(h)
    x = x + _dot(o_scr[...], wo_ref[...])
    u = _rms(x, nffn_ref[...]).astype(BF16)
    for c0 in range(0, FFN_HIDDEN, PROJ_COLS):
        c1 = min(c0 + PROJ_COLS, FFN_HIDDEN)
        gte = _dot(u, wg_ref[:, c0:c1])
        up = _dot(u, wu_ref[:, c0:c1])
        h_scr[:, c0:c1] = (gte * jax.nn.sigmoid(gte) * up).astype(BF16)
    x = x + _dot(h_scr[...], wd_ref[...])
    out_ref[0] = _rms(x, nfin_ref[...])


def _tail(x3d, kv, norm_cross, w_q, w_o, norm_ffn, w_gate, w_up, w_down, norm_final):
    b, seq_len, _ = x3d.shape
    m = kv.shape[1]
    consts = (norm_cross, w_q, w_o, norm_ffn, w_gate, w_up, w_down, norm_final)
    return pl.pallas_call(
        _tail_kernel,
        grid=(b, seq_len // ROW_TILE),
        in_specs=[pl.BlockSpec((1, ROW_TILE, D_MODEL), lambda bi, i: (bi, i, 0)),
                  pl.BlockSpec((1, m, 2 * D_MODEL), lambda bi, i: (bi, 0, 0))]
        + [_const_spec(a.shape) for a in consts],
        out_specs=pl.BlockSpec((1, ROW_TILE, D_MODEL), lambda bi, i: (bi, i, 0)),
        out_shape=jax.ShapeDtypeStruct((b, seq_len, D_MODEL), F32),
        scratch_shapes=[pltpu.VMEM((ROW_TILE, D_MODEL), BF16), pltpu.VMEM((ROW_TILE, FFN_HIDDEN), BF16),
                        pltpu.VMEM((ROW_TILE, D_MODEL), BF16), pltpu.VMEM((X_HEADS, ROW_TILE, m), F32)],
        compiler_params=_params(2),
        name="cross_ffn",
    )(x3d, kv, *consts)


def _prepare(norm_mix, w_in, conv_w, conv_b, attn_sink, a_log, dt_bias, d_skip, ssm_norm, w_br_attn,
             w_br_ssm, w_out, norm_cross, norm_mem, w_q_cross, w_kv_cross, w_o_cross, norm_ffn, w_gate_up,
             w_down, norm_final):
    row = lambda v: v.reshape(1, -1).astype(F32)
    w = w_in[0]
    cuts = np.cumsum([0, ATT_WIDTH + 2 * KV_WIDTH, SSM_INNER, CONV_DIM, 2 * SSM_HEADS, 2 * D_MODEL])
    wqkv, wz, wxbc, wdt, wgate = (w[:, int(s):int(e)].astype(BF16) for s, e in zip(cuts[:-1], cuts[1:]))
    order = np.asarray(_ATT_HEAD_ORDER)
    head_cols = (order[:, None] * HEAD_DIM + np.arange(HEAD_DIM)[None, :]).reshape(-1)
    wqkv = jnp.concatenate([wqkv[:, head_cols], wqkv[:, ATT_WIDTH:]], axis=1)
    slopes = jnp.exp2(-8.0 * jnp.arange(1, ATT_HEADS + 1, dtype=F32) / ATT_HEADS)
    return dict(
        norm_mix=row(norm_mix[0]), wqkv=wqkv, wz=wz, wxbc=wxbc, wdt=wdt, wgate=wgate,
        conv_w=conv_w[0].astype(F32), conv_b=row(conv_b[0]),
        slopes=slopes[order] * LOG2E, sink=attn_sink[0].astype(F32)[order] * LOG2E,
        a_neg=row(-jnp.exp(a_log[0].astype(F32))), dt_bias=row(dt_bias[0]),
        d_skip=row(jnp.repeat(d_skip[0].astype(F32), SSM_HEAD_DIM)), ssm_norm=row(ssm_norm[0]),
        w_br_attn=w_br_attn[0][head_cols, :].astype(BF16), w_br_ssm=w_br_ssm[0].astype(BF16),
        w_out=w_out[0].astype(BF16),
        norm_cross=row(norm_cross[0]), norm_mem=row(norm_mem[0]),
        w_q=w_q_cross[0].astype(BF16), w_kv=w_kv_cross[0].astype(BF16), w_o=w_o_cross[0].astype(BF16),
        norm_ffn=row(norm_ffn[0]), w_gate=w_gate_up[0][:, :FFN_HIDDEN].astype(BF16),
        w_up=w_gate_up[0][:, FFN_HIDDEN:].astype(BF16), w_down=w_down[0].astype(BF16),
        norm_final=row(norm_final), consts=_ssd_constants())


def _trunk(x, mem, p):
    b, seq_len, _ = x.shape
    assert seq_len % ROW_TILE == 0 and seq_len % ATT_TQ == 0 and seq_len % CHUNK == 0
    qkv, z, xbc, dt_raw, gate = _inproj(x.reshape(b * seq_len, D_MODEL), p["norm_mix"], p["wqkv"], p["wz"],
                                        p["wxbc"], p["wdt"], p["wgate"])
    as3d = lambda t: t.reshape(b, seq_len, t.shape[-1])
    qkv, z, xbc, dt_raw, gate = as3d(qkv), as3d(z), as3d(xbc), as3d(dt_raw), as3d(gate)
    attn_part = _attention(qkv, gate, p["slopes"], p["sink"], p["w_br_attn"], b, seq_len)
    xconv, s_in, tcol, trow = _ssd_fwd(xbc, dt_raw, p["conv_w"], p["conv_b"], p["dt_bias"], p["a_neg"],
                                       p["consts"], b, seq_len)
    x1 = _ssd_main(xconv, tcol, trow, z, s_in, attn_part, gate, x, p["consts"], p["d_skip"], p["ssm_norm"],
                   p["w_br_ssm"], p["w_out"], b, seq_len)
    kv = _memkv(mem, p["norm_mem"], p["w_kv"])
    return _tail(x1, kv, p["norm_cross"], p["w_q"], p["w_o"], p["norm_ffn"], p["w_gate"], p["w_up"],
                 p["w_down"], p["norm_final"])


def kernel(x_prompt, x_sample, mem_prompt, mem_sample, norm_mix, w_in, conv_w, conv_b, attn_sink, a_log,
           dt_bias, d_skip, ssm_norm, w_br_attn, w_br_ssm, w_out, norm_cross, norm_mem, w_q_cross,
           w_kv_cross, w_o_cross, norm_ffn, w_gate_up, w_down, norm_final):
    p = _prepare(norm_mix, w_in, conv_w, conv_b, attn_sink, a_log, dt_bias, d_skip, ssm_norm, w_br_attn,
                 w_br_ssm, w_out, norm_cross, norm_mem, w_q_cross, w_kv_cross, w_o_cross, norm_ffn,
                 w_gate_up, w_down, norm_final)
    return (_trunk(x_prompt, mem_prompt, p), _trunk(x_sample, mem_sample, p))
```

```python
import functools

import numpy as np
import jax
import jax.numpy as jnp
from jax import lax
from jax.experimental import pallas as pl
from jax.experimental.pallas import tpu as pltpu

F32 = jnp.float32
BF16 = jnp.bfloat16

D_MODEL = 1024
EPS = 1e-6
ATT_HEADS = 16
ATT_KV_HEADS = 4
ATT_REP = ATT_HEADS // ATT_KV_HEADS
HEAD_DIM = 64
ATT_WIDTH = ATT_HEADS * HEAD_DIM
KV_WIDTH = ATT_KV_HEADS * HEAD_DIM
WINDOW = 128
SSM_INNER = 2 * D_MODEL
SSM_HEAD_DIM = 64
SSM_HEADS = SSM_INNER // SSM_HEAD_DIM
SSM_GROUPS = 4
SSM_REP = SSM_HEADS // SSM_GROUPS
SSM_STATE = 128
SSM_GN = SSM_GROUPS * SSM_STATE
CONV_WIDTH = 5
CONV_DIM = SSM_INNER + 2 * SSM_GN
CHUNK = 128
GROUP_WIDTH = SSM_REP * SSM_HEAD_DIM
X_HEADS = 4
X_HEAD_DIM = D_MODEL // X_HEADS
FFN_HIDDEN = -(-8 * D_MODEL // (3 * 256)) * 256

LANES = 128
BF16_SUBLANES = 16
VMEM_LIMIT_BYTES = 56 * 1024 * 1024

ROW_TILE = 512
ATT_TQ = 256
ATT_SKEW = 4
PROJ_COLS = 512
CONV_COLS = 512

LOG2E = 1.4426950408889634
Q_SCALE = HEAD_DIM ** -0.5 * LOG2E
XQ_SCALE = X_HEAD_DIM ** -0.5 * LOG2E


def _rms(x, g):
    return x * lax.rsqrt(jnp.mean(x * x, axis=-1, keepdims=True) + EPS) * g


def _const_spec(shape):
    nd = len(shape)
    return pl.BlockSpec(shape, lambda *_: (0,) * nd, pipeline_mode=pl.Buffered(1))


def _params(n_grid):
    return pltpu.CompilerParams(dimension_semantics=("arbitrary",) * n_grid,
                                vmem_limit_bytes=VMEM_LIMIT_BYTES)


def _dot(a, b):
    return jnp.dot(a, b, preferred_element_type=F32)


def _dot_nt(a, b):
    return lax.dot_general(a, b, (((1,), (1,)), ((), ())), preferred_element_type=F32)


def _dot_tn(a, b):
    return lax.dot_general(a, b, (((0,), (0,)), ((), ())), preferred_element_type=F32)


def _split3(v):
    h1 = v.astype(BF16)
    r1 = v - h1.astype(F32)
    h2 = r1.astype(BF16)
    r2 = r1 - h2.astype(F32)
    return h1, h2, r2.astype(BF16)


def _exact_dot(sel, v):
    h1, h2, h3 = _split3(v)
    return _dot(sel, h1) + _dot(sel, h2) + _dot(sel, h3)


def _expand(v, e2_ref):
    hi = v.astype(BF16)
    lo = (v - hi.astype(F32)).astype(BF16)
    return _dot(jnp.concatenate([hi, lo], axis=1), e2_ref[...])


def _inproj_kernel(x_ref, g_ref, wq_ref, wkv_ref, wz_ref, wxbc_ref, wdt_ref, wgate_ref,
                   qkv_ref, z_ref, xbc_ref, dt_ref, gate_ref):
    u = _rms(x_ref[...], g_ref[...]).astype(BF16)

    def proj(w_ref, o_ref, post):
        n = w_ref.shape[1]
        for c0 in range(0, n, PROJ_COLS):
            c1 = min(c0 + PROJ_COLS, n)
            o_ref[:, c0:c1] = post(_dot(u, w_ref[:, c0:c1])).astype(o_ref.dtype)

    ident = lambda t: t
    for c0 in range(0, ATT_WIDTH, PROJ_COLS):
        qkv_ref[:, c0:c0 + PROJ_COLS] = (_dot(u, wq_ref[:, c0:c0 + PROJ_COLS]) * Q_SCALE).astype(BF16)
    qkv_ref[:, ATT_WIDTH:] = _dot(u, wkv_ref[...]).astype(BF16)
    proj(wz_ref, z_ref, lambda t: t * jax.nn.sigmoid(t))
    proj(wxbc_ref, xbc_ref, ident)
    proj(wdt_ref, dt_ref, ident)
    proj(wgate_ref, gate_ref, jax.nn.sigmoid)


def _inproj(x2d, norm_mix, wq, wkv, wz, wxbc, wdt, wgate):
    t = x2d.shape[0]
    row = lambda n: pl.BlockSpec((ROW_TILE, n), lambda i: (i, 0))
    widths = (wq.shape[1] + wkv.shape[1], wz.shape[1], wxbc.shape[1], wdt.shape[1], wgate.shape[1])
    dtypes = (BF16, BF16, BF16, F32, BF16)
    return pl.pallas_call(
        _inproj_kernel,
        grid=(t // ROW_TILE,),
        in_specs=[row(D_MODEL), _const_spec((1, D_MODEL))]
        + [_const_spec(w.shape) for w in (wq, wkv, wz, wxbc, wdt, wgate)],
        out_specs=[row(n) for n in widths],
        out_shape=[jax.ShapeDtypeStruct((t, n), dt) for n, dt in zip(widths, dtypes)],
        compiler_params=_params(1),
        name="inproj",
    )(x2d, norm_mix, wq, wkv, wz, wxbc, wdt, wgate)


def _attn_kernel(seq_len, q_ref, kp_ref, kc_ref, kn_ref, vp_ref, vc_ref, vn_ref, half_ref, slope_ref,
                 sink_ref, wbr_ref, gate_ref, out_ref, ot_scr, s_scr):
    i = pl.program_id(1)
    tk = ATT_TQ + 2 * WINDOW
    j = lax.broadcasted_iota(jnp.int32, (tk, ATT_TQ), 0)
    r = lax.broadcasted_iota(jnp.int32, (tk, ATT_TQ), 1)
    dist = jnp.abs(r + WINDOW - j)
    kpos = i * ATT_TQ - WINDOW + j
    valid = (dist <= WINDOW) & (kpos >= 0) & (kpos < seq_len)
    mdist = jnp.where(valid, dist.astype(F32), jnp.inf)
    k = jnp.concatenate([kp_ref[0], kc_ref[0], kn_ref[0]], axis=0)
    v = jnp.concatenate([vp_ref[0], vc_ref[0], vn_ref[0]], axis=0)
    k_halves, v2t = [], []
    for pair in range(ATT_KV_HEADS // 2):
        k2 = k[:, pair * LANES:(pair + 1) * LANES]
        k_halves.append((k2 * half_ref[0:1, :], k2 * half_ref[1:2, :]))
        v2t.append(v[:, pair * LANES:(pair + 1) * LANES].T)

    def logits(pos):
        tile, half = pos // 2, pos % 2
        qt = q_ref[0, :, tile * LANES:(tile + 1) * LANES]
        s_scr[pos] = _dot_nt(k_halves[tile // ATT_REP][half], qt) - slope_ref[pos] * mdist

    def attend(pos):
        tile, half = pos // 2, pos % 2
        s = s_scr[pos]
        sink = sink_ref[pos]
        m = jnp.maximum(jnp.max(s, axis=0, keepdims=True), sink)
        p = jnp.exp2(s - m)
        denom = jnp.sum(p, axis=0, keepdims=True) + jnp.exp2(sink - m)
        vt = v2t[tile // ATT_REP][half * HEAD_DIM:(half + 1) * HEAD_DIM]
        ot_scr[pos * HEAD_DIM:(pos + 1) * HEAD_DIM, :] = (_dot(vt, p.astype(BF16)) / denom).astype(BF16)

    for pos in range(ATT_HEADS + ATT_SKEW):
        if pos < ATT_HEADS:
            logits(pos)
        if pos >= ATT_SKEW:
            attend(pos - ATT_SKEW)
    y = _dot_tn(ot_scr[...], wbr_ref[...])
    out_ref[0] = (gate_ref[0].astype(F32) * y).astype(out_ref.dtype)


_ATT_HEAD_ORDER = [ATT_REP * (2 * (p // (2 * ATT_REP)) + p % 2) + (p // 2) % ATT_REP for p in range(ATT_HEADS)]


def _attention(qkv, gate, slopes, sink, w_br_attn, b, seq_len):
    nq = seq_len // ATT_TQ
    halo_per_q = ATT_TQ // WINDOW
    n_halo = seq_len // WINDOW
    k_col, v_col = ATT_WIDTH // KV_WIDTH, ATT_WIDTH // KV_WIDTH + 1
    prev = lambda col: pl.BlockSpec(
        (1, WINDOW, KV_WIDTH), lambda bi, i: (bi, jnp.maximum(i * halo_per_q - 1, 0), col))
    cur = lambda col: pl.BlockSpec((1, ATT_TQ, KV_WIDTH), lambda bi, i: (bi, i, col))
    nxt = lambda col: pl.BlockSpec(
        (1, WINDOW, KV_WIDTH), lambda bi, i: (bi, jnp.minimum((i + 1) * halo_per_q, n_halo - 1), col))
    smem = pl.BlockSpec(memory_space=pltpu.SMEM)
    lane_half = np.arange(LANES) // HEAD_DIM
    half_masks = jnp.asarray(np.stack([lane_half == 0, lane_half == 1]), BF16)
    return pl.pallas_call(
        functools.partial(_attn_kernel, seq_len),
        grid=(b, nq),
        in_specs=[pl.BlockSpec((1, ATT_TQ, ATT_WIDTH), lambda bi, i: (bi, i, 0)),
                  prev(k_col), cur(k_col), nxt(k_col), prev(v_col), cur(v_col), nxt(v_col),
                  _const_spec(half_masks.shape), smem, smem, _const_spec(w_br_attn.shape),
                  pl.BlockSpec((1, ATT_TQ, D_MODEL), lambda bi, i: (bi, i, 0))],
        out_specs=pl.BlockSpec((1, ATT_TQ, D_MODEL), lambda bi, i: (bi, i, 0)),
        out_shape=jax.ShapeDtypeStruct((b, seq_len, D_MODEL), BF16),
        scratch_shapes=[pltpu.VMEM((ATT_WIDTH, ATT_TQ), BF16),
                        pltpu.VMEM((ATT_HEADS, ATT_TQ + 2 * WINDOW, ATT_TQ), F32)],
        compiler_params=_params(2),
        name="window_attn",
    )(qkv, qkv, qkv, qkv, qkv, qkv, qkv, half_masks, slopes, sink, w_br_attn, gate)


CONV_HALO = CONV_WIDTH // 2


def _dt_terms(dt_ref, dtb_ref, a_ref, tril_ref, triu_ref):
    dt = jax.nn.softplus(dt_ref[0] + dtb_ref[...])
    dta = dt * a_ref[...]
    h1, h2, h3 = _split3(dta)
    pre = _dot(tril_ref[...], h1) + _dot(tril_ref[...], h2) + _dot(tril_ref[...], h3)
    suf = _dot(triu_ref[...], h1) + _dot(triu_ref[...], h2) + _dot(triu_ref[...], h3)
    lane = lax.broadcasted_iota(jnp.int32, dt.shape, 1)
    cum = jnp.where(lane < SSM_HEADS, pre, suf)
    return dt, cum


def _ssd_fwd_kernel(xp_ref, xc_ref, xn_ref, shift_ref, cw_ref, cb_ref, dt_ref, dtb_ref, a_ref, tril_ref,
                    triu_ref, e2_ref, xconv_ref, sin_ref, tcol_ref, trow_ref, state_scr):
    c = pl.program_id(1)

    @pl.when(c == 0)
    def _():
        state_scr[...] = jnp.zeros_like(state_scr)

    dt, cum = _dt_terms(dt_ref, dtb_ref, a_ref, tril_ref, triu_ref)

    taps_off = [kk for kk in range(CONV_WIDTH) if kk != CONV_HALO]
    for c0 in range(0, CONV_DIM, CONV_COLS):
        cols = slice(c0, c0 + CONV_COLS)
        cur = xc_ref[0, :, cols]
        ext = jnp.concatenate([xp_ref[0, :, cols], cur, xn_ref[0, :, cols]], axis=0)
        taps = jnp.concatenate([ext * cw_ref[kk:kk + 1, cols].astype(BF16) for kk in taps_off]
                               + [cur * cw_ref[CONV_HALO:CONV_HALO + 1, cols].astype(BF16)], axis=0)
        acc = _dot(shift_ref[0], taps) + cb_ref[:, cols]
        xconv_ref[0, :, cols] = (acc * jax.nn.sigmoid(acc)).astype(BF16)

    total = cum[CHUNK - 1:CHUNK, :]
    first = cum[0:1, :]
    lane64 = lax.broadcasted_iota(jnp.int32, dt.shape, 1)
    fwd_lane = lane64 < SSM_HEADS

    pad = jnp.zeros((CHUNK, LANES - 2 * SSM_HEADS), F32)
    wide = lambda t: jnp.concatenate([t, pad[:t.shape[0]]], axis=1)
    cum2 = cum * LOG2E
    tcol_ref[0, 0, 0:CHUNK] = wide(cum2)
    tcol_ref[0, 0, CHUNK:2 * CHUNK] = wide(jnp.exp(cum))
    tcol_ref[0, 0, 2 * CHUNK:3 * CHUNK] = wide(jnp.exp(jnp.where(fwd_lane, 0.0, first - cum)) * dt)
    tcol_ref[0, 0, 3 * CHUNK:] = wide(jnp.broadcast_to(jnp.exp(first), (8, 2 * SSM_HEADS)))
    row_t = wide(cum2 - jnp.log2(dt)).T
    dt_t = wide(dt).T
    diag_t = jnp.log2(dt_t[0:SSM_HEADS] + dt_t[SSM_HEADS:2 * SSM_HEADS])
    full = jnp.concatenate([row_t[0:2 * SSM_HEADS], diag_t,
                            jnp.zeros((LANES - 3 * SSM_HEADS, CHUNK), F32)], axis=0)
    swapped = pltpu.roll(full, CHUNK // 2, axis=1)
    next_row = lambda t: pltpu.roll(t, LANES - 1, axis=0)
    low = lax.broadcasted_iota(jnp.int32, full.shape, 1) < CHUNK // 2
    trow_ref[0, 0, 0] = jnp.where(low, full, next_row(swapped))
    trow_ref[0, 0, 1] = jnp.where(low, swapped, next_row(full))

    w = jnp.exp(jnp.where(fwd_lane, total - cum, 0.0)) * dt
    wexp = _expand(w, e2_ref)
    cdec = _expand(jnp.broadcast_to(jnp.exp(total), (8, 2 * SSM_HEADS)), e2_ref)[0:1, :]
    for g in range(SSM_GROUPS):
        lo, hi = g * GROUP_WIDTH, (g + 1) * GROUP_WIDTH
        xw = (xconv_ref[0, :, lo:hi].astype(F32) * wexp[:, lo:hi]).astype(BF16)
        bg = xconv_ref[0, :, SSM_INNER + g * SSM_STATE:SSM_INNER + (g + 1) * SSM_STATE]
        st = state_scr[g]
        sin_ref[0, 0, g] = st.astype(BF16)
        state_scr[g] = st * cdec[:, lo:hi] + _dot_tn(bg, xw)


def _ssd_main_kernel(nc, xc_ref, tcol_ref, trow_ref, z_ref, sin_ref, attn_ref, gate_ref, xres_ref,
                     half_ref, e2f_ref, e2b_ref, dskip_ref, snorm_ref,
                     wbr_ref, wout_ref, out_ref, y_scr, state_scr, exp_scr, yn_scr, cb_scr, m_scr):
    c = pl.program_id(1)

    @pl.when(c == 0)
    def _():
        state_scr[...] = jnp.zeros_like(state_scr)
        yn_scr[...] = jnp.zeros_like(yn_scr)

    def project():
        y_ssm = _dot(yn_scr[...], wbr_ref[...])
        merged = attn_ref[0].astype(F32) + gate_ref[0].astype(F32) * y_ssm
        out_ref[0] = xres_ref[0] + _dot(merged.astype(BF16), wout_ref[...])

    pl.when(c == nc)(project)

    @pl.when(c < nc)
    def _():
        _ssd_chunk(project, xc_ref, tcol_ref, trow_ref, z_ref, sin_ref, half_ref, e2f_ref, e2b_ref, dskip_ref,
                   snorm_ref, y_scr, state_scr, exp_scr, yn_scr, cb_scr, m_scr)


def _ssd_chunk(project, xc_ref, tcol_ref, trow_ref, z_ref, sin_ref, half_ref, e2f_ref, e2b_ref, dskip_ref,
               snorm_ref, y_scr, state_scr, exp_scr, yn_scr, cb_scr, m_scr):
    half = CHUNK // 2
    low = lax.broadcasted_iota(jnp.int32, (CHUNK, LANES), 1) < half

    def b_c(g):
        return (xc_ref[0, :, SSM_INNER + g * SSM_STATE:SSM_INNER + (g + 1) * SSM_STATE],
                xc_ref[0, :, SSM_INNER + SSM_GN + g * SSM_STATE:SSM_INNER + SSM_GN + (g + 1) * SSM_STATE])

    for g in range(SSM_GROUPS):
        bg, cg = b_c(g)
        cb = _dot_nt(cg, bg)
        cb_sw = pltpu.roll(cb, half, axis=1)
        cb_scr[g, 0] = jnp.where(low, cb, cb_sw)
        cb_scr[g, 1] = jnp.where(low, cb_sw, cb)

    project()
    exp_scr[0] = _expand(tcol_ref[0, 0, CHUNK:2 * CHUNK], e2f_ref)
    exp_scr[1] = _expand(tcol_ref[0, 0, CHUNK:2 * CHUNK], e2b_ref)
    exp_scr[2] = _expand(tcol_ref[0, 0, 2 * CHUNK:3 * CHUNK], e2b_ref)
    cdec = _expand(tcol_ref[0, 0, 3 * CHUNK:], e2b_ref)[0:1, :]

    cum2 = tcol_ref[0, 0, 0:CHUNK]
    src = lax.broadcasted_iota(jnp.int32, (half, LANES), 1) % half
    row = lax.broadcasted_iota(jnp.int32, (half, LANES), 0)
    lower, upper = row > src, row < src
    top, bot = slice(0, half), slice(half, CHUNK)

    def decay_matrices(g):
        for pr in range(SSM_REP // 2):
            h = g * SSM_REP + 2 * pr
            hb = SSM_HEADS + h
            col_f = jnp.take_along_axis(cum2, jnp.where(low, h, h + 1), axis=1)
            col_b = jnp.take_along_axis(cum2, jnp.where(low, hb, hb + 1), axis=1)
            row_f = [trow_ref[0, 0, j, h:h + 1, :] for j in range(2)]
            row_b = [trow_ref[0, 0, j, hb:hb + 1, :] for j in range(2)]
            diag = [trow_ref[0, 0, j, 2 * SSM_HEADS + h:2 * SSM_HEADS + h + 1, :] for j in range(2)]
            mixed = lambda rows, j: jnp.where(lower, col_f[rows] - row_f[j],
                                              jnp.where(upper, col_b[rows] - row_b[j], diag[j]))
            args = [[mixed(top, 0), col_b[top] - row_b[1]],
                    [col_f[bot] - row_f[0], mixed(bot, 1)]]
            for j in range(2):
                for i, rows in enumerate((top, bot)):
                    m_scr[h // 2, rows, j * LANES:(j + 1) * LANES] = (
                        cb_scr[g, j, rows, :] * jnp.exp2(args[i][j])).astype(BF16)

    def inter(g):
        lo, hi = g * GROUP_WIDTH, (g + 1) * GROUP_WIDTH
        bg, cg = b_c(g)
        st = state_scr[g]
        xg = xc_ref[0, :, lo:hi].astype(F32)
        y_scr[:, lo:hi] = (_dot(cg, sin_ref[0, 0, g]) * exp_scr[0, :, lo:hi]
                           + _dot(cg, st.astype(BF16)) * exp_scr[1, :, lo:hi] + dskip_ref[:, lo:hi] * xg)
        xw = (xg * exp_scr[2, :, lo:hi]).astype(BF16)
        state_scr[g] = st * cdec[:, lo:hi] + _dot_tn(bg, xw)

    def intra(g):
        m0, m1 = half_ref[0:1, :], half_ref[1:2, :]
        for pr in range(SSM_REP // 2):
            h = g * SSM_REP + 2 * pr
            cols = slice(h * SSM_HEAD_DIM, h * SSM_HEAD_DIM + LANES)
            xa = xc_ref[0, 0:half, cols]
            xb = xc_ref[0, half:, cols]
            xdiag = jnp.concatenate([xa * m0, xa * m1, xb * m0, xb * m1], axis=0)
            y_scr[:, cols] = (y_scr[:, cols] + _dot(m_scr[h // 2], xdiag)) * z_ref[0, :, cols].astype(F32)

    for g in range(SSM_GROUPS):
        inter(g)
    for g in range(SSM_GROUPS):
        decay_matrices(g)
    for g in range(SSM_GROUPS):
        intra(g)

    yn_scr[...] = _rms(y_scr[...], snorm_ref[...]).astype(BF16)


def _ssd_constants():
    idx = np.arange(CHUNK)
    tril = (idx[:, None] >= idx[None, :]).astype(np.float32)
    heads = np.arange(2 * SSM_HEADS)
    chan_head = np.arange(SSM_INNER) // SSM_HEAD_DIM
    e_f = (heads[:, None] == chan_head[None, :]).astype(np.float32)
    e_b = (heads[:, None] == chan_head[None, :] + SSM_HEADS).astype(np.float32)
    stack = lambda e: jnp.asarray(np.concatenate([e, e], axis=0), BF16)
    widen = lambda e: np.concatenate([e, np.zeros((LANES - e.shape[0], e.shape[1]), np.float32)], axis=0)
    ext_rows = CHUNK + 2 * BF16_SUBLANES
    taps_off = [kk for kk in range(CONV_WIDTH) if kk != CONV_HALO]
    shift = np.zeros((4, CHUNK, len(taps_off) * ext_rows + CHUNK), np.float32)
    shift[:, idx, len(taps_off) * ext_rows + idx] = 1.0
    for n, kk in enumerate(taps_off):
        shift[:, idx, n * ext_rows + BF16_SUBLANES + idx + kk - CONV_HALO] = 1.0
        lo_halo = slice(n * ext_rows, n * ext_rows + BF16_SUBLANES)
        hi_halo = slice(n * ext_rows + BF16_SUBLANES + CHUNK, (n + 1) * ext_rows)
        shift[0, :, lo_halo] = 0.0
        shift[1, :, lo_halo] = 0.0
        shift[0, :, hi_halo] = 0.0
        shift[2, :, hi_halo] = 0.0
    lane_half = np.arange(LANES) // SSM_HEAD_DIM
    return dict(tril=jnp.asarray(tril, BF16), triu=jnp.asarray(tril.T, BF16), e2f=stack(e_f),
                e2f_wide=stack(widen(e_f)), e2b_wide=stack(widen(e_b)), shift=jnp.asarray(shift, BF16),
                half_masks=jnp.asarray(np.stack([lane_half == 0, lane_half == 1]), BF16))


TCOL_ROWS = 3 * CHUNK + 8


def _ssd_fwd(xbc, dt_raw, conv_w, conv_b, dt_bias, a_neg, consts, b, seq_len):
    tril, triu, e2f, shift = consts["tril"], consts["triu"], consts["e2f"], consts["shift"]
    nc = seq_len // CHUNK
    halo_blocks = CHUNK // BF16_SUBLANES
    n_halo = seq_len // BF16_SUBLANES
    shift_spec = pl.BlockSpec(
        (1,) + shift.shape[1:],
        lambda bi, c: (2 * (c > 0).astype(jnp.int32) + (c < nc - 1).astype(jnp.int32), 0, 0))
    return pl.pallas_call(
        _ssd_fwd_kernel,
        grid=(b, nc),
        in_specs=[pl.BlockSpec((1, BF16_SUBLANES, CONV_DIM),
                               lambda bi, c: (bi, jnp.maximum(c * halo_blocks - 1, 0), 0)),
                  pl.BlockSpec((1, CHUNK, CONV_DIM), lambda bi, c: (bi, c, 0)),
                  pl.BlockSpec((1, BF16_SUBLANES, CONV_DIM),
                               lambda bi, c: (bi, jnp.minimum((c + 1) * halo_blocks, n_halo - 1), 0)),
                  shift_spec, _const_spec(conv_w.shape), _const_spec(conv_b.shape),
                  pl.BlockSpec((1, CHUNK, 2 * SSM_HEADS), lambda bi, c: (bi, c, 0)),
                  _const_spec(dt_bias.shape), _const_spec(a_neg.shape),
                  _const_spec(tril.shape), _const_spec(triu.shape), _const_spec(e2f.shape)],
        out_specs=[pl.BlockSpec((1, CHUNK, CONV_DIM), lambda bi, c: (bi, c, 0)),
                   pl.BlockSpec((1, 1, SSM_GROUPS, SSM_STATE, GROUP_WIDTH), lambda bi, c: (bi, c, 0, 0, 0)),
                   pl.BlockSpec((1, 1, TCOL_ROWS, LANES), lambda bi, c: (bi, c, 0, 0)),
                   pl.BlockSpec((1, 1, 2, LANES, CHUNK), lambda bi, c: (bi, c, 0, 0, 0))],
        out_shape=[jax.ShapeDtypeStruct((b, seq_len, CONV_DIM), BF16),
                   jax.ShapeDtypeStruct((b, nc, SSM_GROUPS, SSM_STATE, GROUP_WIDTH), BF16),
                   jax.ShapeDtypeStruct((b, nc, TCOL_ROWS, LANES), F32),
                   jax.ShapeDtypeStruct((b, nc, 2, LANES, CHUNK), F32)],
        scratch_shapes=[pltpu.VMEM((SSM_GROUPS, SSM_STATE, GROUP_WIDTH), F32)],
        compiler_params=_params(2),
        name="ssd_fwd",
    )(xbc, xbc, xbc, shift, conv_w, conv_b, dt_raw, dt_bias, a_neg, tril, triu, e2f)


def _ssd_main(xconv, tcol, trow, z, s_in, attn_part, gate, x3d, consts, d_skip_exp, ssm_norm, w_br_ssm, w_out,
              b, seq_len):
    nc = seq_len // CHUNK
    ssd_chunk = lambda c: nc - 1 - jnp.minimum(c, nc - 1)
    proj_chunk = lambda c: nc - 1 - jnp.maximum(c - 1, 0)
    rev = lambda width: pl.BlockSpec((1, CHUNK, width), lambda bi, c: (bi, ssd_chunk(c), 0))
    rev_chunk = lambda shape: pl.BlockSpec((1, 1) + shape, lambda bi, c: (bi, ssd_chunk(c)) + (0,) * len(shape))
    late = lambda col=0: pl.BlockSpec((1, CHUNK, D_MODEL), lambda bi, c: (bi, proj_chunk(c), col))
    small = (consts["half_masks"], consts["e2f_wide"], consts["e2b_wide"], d_skip_exp, ssm_norm, w_br_ssm, w_out)
    return pl.pallas_call(
        functools.partial(_ssd_main_kernel, nc),
        grid=(b, nc + 1),
        in_specs=[rev(CONV_DIM), rev_chunk((TCOL_ROWS, LANES)), rev_chunk((2, LANES, CHUNK)), rev(SSM_INNER),
                  rev_chunk((SSM_GROUPS, SSM_STATE, GROUP_WIDTH)), late(), late(1), late()]
        + [_const_spec(a.shape) for a in small],
        out_specs=late(),
        out_shape=jax.ShapeDtypeStruct((b, seq_len, D_MODEL), F32),
        scratch_shapes=[pltpu.VMEM((CHUNK, SSM_INNER), F32),
                        pltpu.VMEM((SSM_GROUPS, SSM_STATE, GROUP_WIDTH), F32),
                        pltpu.VMEM((3, CHUNK, SSM_INNER), F32),
                        pltpu.VMEM((CHUNK, SSM_INNER), BF16),
                        pltpu.VMEM((SSM_GROUPS, 2, CHUNK, LANES), F32),
                        pltpu.VMEM((SSM_HEADS // 2, CHUNK, 2 * LANES), BF16)],
        compiler_params=_params(2),
        name="ssd_main",
    )(xconv, tcol, trow, z, s_in, attn_part, gate, x3d, *small)


def _memkv_kernel(mem_ref, g_ref, wkv_ref, kv_ref):
    mn = _rms(mem_ref[0], g_ref[...]).astype(BF16)
    for c0 in range(0, 2 * D_MODEL, PROJ_COLS):
        kv_ref[0, :, c0:c0 + PROJ_COLS] = _dot(mn, wkv_ref[:, c0:c0 + PROJ_COLS]).astype(BF16)


def _memkv(mem, norm_mem, w_kv):
    b, m, _ = mem.shape
    return pl.pallas_call(
        _memkv_kernel,
        grid=(b,),
        in_specs=[pl.BlockSpec((1, m, D_MODEL), lambda bi: (bi, 0, 0)),
                  _const_spec(norm_mem.shape), _const_spec(w_kv.shape)],
        out_specs=pl.BlockSpec((1, m, 2 * D_MODEL), lambda bi: (bi, 0, 0)),
        out_shape=jax.ShapeDtypeStruct((b, m, 2 * D_MODEL), BF16),
        compiler_params=_params(1),
        name="mem_kv",
    )(mem, norm_mem, w_kv)


def _tail_kernel(x_ref, kv_ref, ncross_ref, wq_ref, wo_ref, nffn_ref, wg_ref, wu_ref, wd_ref, nfin_ref,
                 out_ref, o_scr, h_scr, q_scr, s_scr):
    x = x_ref[0]
    u = _rms(x, ncross_ref[...]).astype(BF16)
    for c0 in range(0, D_MODEL, PROJ_COLS):
        q_scr[:, c0:c0 + PROJ_COLS] = (_dot(u, wq_ref[:, c0:c0 + PROJ_COLS]) * XQ_SCALE).astype(BF16)

    def logits(h):
        lo, hi = h * X_HEAD_DIM, (h + 1) * X_HEAD_DIM
        s_scr[h] = _dot_nt(q_scr[:, lo:hi], kv_ref[0, :, lo:hi])

    def attend(h):
        lo, hi = h * X_HEAD_DIM, (h + 1) * X_HEAD_DIM
        s = s_scr[h]
        p = jnp.exp2(s - jnp.max(s, axis=-1, keepdims=True))
        denom = jnp.sum(p, axis=-1, keepdims=True)
        o = _dot(p.astype(BF16), kv_ref[0, :, D_MODEL + lo:D_MODEL + hi]) / denom
        o_scr[:, lo:hi] = o.astype(BF16)

    logits(0)
    for h in range(X_HEADS):
        if h + 1 < X_HEADS:
            logits(h + 1)
        attend(h)
    x = x + _dot(o_scr[...], wo_ref[...])
    u = _rms(x, nffn_ref[...]).astype(BF16)
    for c0 in range(0, FFN_HIDDEN, PROJ_COLS):
        c1 = min(c0 + PROJ_COLS, FFN_HIDDEN)
        gte = _dot(u, wg_ref[:, c0:c1])
        up = _dot(u, wu_ref[:, c0:c1])
        h_scr[:, c0:c1] = (gte * jax.nn.sigmoid(gte) * up).astype(BF16)
    x = x + _dot(h_scr[...], wd_ref[...])
    out_ref[0] = _rms(x, nfin_ref[...])


def _tail(x3d, kv, norm_cross, w_q, w_o, norm_ffn, w_gate, w_up, w_down, norm_final):
    b, seq_len, _ = x3d.shape
    m = kv.shape[1]
    consts = (norm_cross, w_q, w_o, norm_ffn, w_gate, w_up, w_down, norm_final)
    return pl.pallas_call(
        _tail_kernel,
        grid=(b, seq_len // ROW_TILE),
        in_specs=[pl.BlockSpec((1, ROW_TILE, D_MODEL), lambda bi, i: (bi, i, 0)),
                  pl.BlockSpec((1, m, 2 * D_MODEL), lambda bi, i: (bi, 0, 0))]
        + [_const_spec(a.shape) for a in consts],
        out_specs=pl.BlockSpec((1, ROW_TILE, D_MODEL), lambda bi, i: (bi, i, 0)),
        out_shape=jax.ShapeDtypeStruct((b, seq_len, D_MODEL), F32),
        scratch_shapes=[pltpu.VMEM((ROW_TILE, D_MODEL), BF16), pltpu.VMEM((ROW_TILE, FFN_HIDDEN), BF16),
                        pltpu.VMEM((ROW_TILE, D_MODEL), BF16), pltpu.VMEM((X_HEADS, ROW_TILE, m), F32)],
        compiler_params=_params(2),
        name="cross_ffn",
    )(x3d, kv, *consts)


def _prepare(norm_mix, w_in, conv_w, conv_b, attn_sink, a_log, dt_bias, d_skip, ssm_norm, w_br_attn,
             w_br_ssm, w_out, norm_cross, norm_mem, w_q_cross, w_kv_cross, w_o_cross, norm_ffn, w_gate_up,
             w_down, norm_final):
    row = lambda v: v.reshape(1, -1).astype(F32)
    w = w_in[0]
    cuts = np.cumsum([0, ATT_WIDTH, 2 * KV_WIDTH, SSM_INNER, CONV_DIM, 2 * SSM_HEADS, 2 * D_MODEL])
    wq, wkv, wz, wxbc, wdt, wgate = (w[:, int(s):int(e)].astype(BF16) for s, e in zip(cuts[:-1], cuts[1:]))
    grid5 = (ATT_KV_HEADS // 2, 2, ATT_REP)
    reorder = lambda t, axis: jnp.swapaxes(
        t.reshape(t.shape[:axis] + grid5 + (HEAD_DIM,) + t.shape[axis + 1:]), axis + 1, axis + 2).reshape(t.shape)
    order = np.asarray(_ATT_HEAD_ORDER)
    slopes = jnp.exp2(-8.0 * jnp.arange(1, ATT_HEADS + 1, dtype=F32) / ATT_HEADS)
    return dict(
        norm_mix=row(norm_mix[0]), wq=reorder(wq, 1), wkv=wkv, wz=wz, wxbc=wxbc, wdt=wdt, wgate=wgate,
        conv_w=conv_w[0].astype(F32), conv_b=row(conv_b[0]),
        slopes=slopes[order] * LOG2E, sink=attn_sink[0].astype(F32)[order] * LOG2E,
        a_neg=row(-jnp.exp(a_log[0].astype(F32))), dt_bias=row(dt_bias[0]),
        d_skip=row(jnp.repeat(d_skip[0].astype(F32), SSM_HEAD_DIM)), ssm_norm=row(ssm_norm[0]),
        w_br_attn=reorder(w_br_attn[0].astype(BF16), 0), w_br_ssm=w_br_ssm[0].astype(BF16),
        w_out=w_out[0].astype(BF16),
        norm_cross=row(norm_cross[0]), norm_mem=row(norm_mem[0]),
        w_q=w_q_cross[0].astype(BF16), w_kv=w_kv_cross[0].astype(BF16), w_o=w_o_cross[0].astype(BF16),
        norm_ffn=row(norm_ffn[0]), w_gate=w_gate_up[0][:, :FFN_HIDDEN].astype(BF16),
        w_up=w_gate_up[0][:, FFN_HIDDEN:].astype(BF16), w_down=w_down[0].astype(BF16),
        norm_final=row(norm_final), consts=_ssd_constants())


def _trunk(x, mem, p):
    b, seq_len, _ = x.shape
    assert seq_len % ROW_TILE == 0 and seq_len % ATT_TQ == 0 and seq_len % CHUNK == 0
    qkv, z, xbc, dt_raw, gate = _inproj(x.reshape(b * seq_len, D_MODEL), p["norm_mix"], p["wq"], p["wkv"],
                                        p["wz"], p["wxbc"], p["wdt"], p["wgate"])
    as3d = lambda t: t.reshape(b, seq_len, t.shape[-1])
    qkv, z, xbc, dt_raw, gate = as3d(qkv), as3d(z), as3d(xbc), as3d(dt_raw), as3d(gate)
    attn_part = _attention(qkv, gate, p["slopes"], p["sink"], p["w_br_attn"], b, seq_len)
    xconv, s_in, tcol, trow = _ssd_fwd(xbc, dt_raw, p["conv_w"], p["conv_b"], p["dt_bias"], p["a_neg"],
                                       p["consts"], b, seq_len)
    x1 = _ssd_main(xconv, tcol, trow, z, s_in, attn_part, gate, x, p["consts"], p["d_skip"], p["ssm_norm"],
                   p["w_br_ssm"], p["w_out"], b, seq_len)
    kv = _memkv(mem, p["norm_mem"], p["w_kv"])
    return _tail(x1, kv, p["norm_cross"], p["w_q"], p["w_o"], p["norm_ffn"], p["w_gate"], p["w_up"],
                 p["w_down"], p["norm_final"])


def kernel(x_prompt, x_sample, mem_prompt, mem_sample, norm_mix, w_in, conv_w, conv_b, attn_sink, a_log,
           dt_bias, d_skip, ssm_norm, w_br_attn, w_br_ssm, w_out, norm_cross, norm_mem, w_q_cross,
           w_kv_cross, w_o_cross, norm_ffn, w_gate_up, w_down, norm_final):
    p = _prepare(norm_mix, w_in, conv_w, conv_b, attn_sink, a_log, dt_bias, d_skip, ssm_norm, w_br_attn,
                 w_br_ssm, w_out, norm_cross, norm_mem, w_q_cross, w_kv_cross, w_o_cross, norm_ffn,
                 w_gate_up, w_down, norm_final)
    return (_trunk(x_prompt, mem_prompt, p), _trunk(x_sample, mem_sample, p))
```

```python
import functools

import numpy as np
import jax
import jax.numpy as jnp
from jax import lax
from jax.experimental import pallas as pl
from jax.experimental.pallas import tpu as pltpu

F32 = jnp.float32
BF16 = jnp.bfloat16

D_MODEL = 1024
EPS = 1e-6
ATT_HEADS = 16
ATT_KV_HEADS = 4
ATT_REP = ATT_HEADS // ATT_KV_HEADS
HEAD_DIM = 64
ATT_WIDTH = ATT_HEADS * HEAD_DIM
KV_WIDTH = ATT_KV_HEADS * HEAD_DIM
WINDOW = 128
SSM_INNER = 2 * D_MODEL
SSM_HEAD_DIM = 64
SSM_HEADS = SSM_INNER // SSM_HEAD_DIM
SSM_GROUPS = 4
SSM_REP = SSM_HEADS // SSM_GROUPS
SSM_STATE = 128
SSM_GN = SSM_GROUPS * SSM_STATE
CONV_WIDTH = 5
CONV_DIM = SSM_INNER + 2 * SSM_GN
CHUNK = 128
GROUP_WIDTH = SSM_REP * SSM_HEAD_DIM
X_HEADS = 4
X_HEAD_DIM = D_MODEL // X_HEADS
FFN_HIDDEN = -(-8 * D_MODEL // (3 * 256)) * 256

LANES = 128
BF16_SUBLANES = 16
VMEM_LIMIT_BYTES = 56 * 1024 * 1024

ROW_TILE = 512
ATT_TQ = 256
ATT_SKEW = 6
PROJ_COLS = 512
CONV_COLS = 512

LOG2E = 1.4426950408889634
Q_SCALE = HEAD_DIM ** -0.5 * LOG2E
XQ_SCALE = X_HEAD_DIM ** -0.5 * LOG2E


def _rms(x, g):
    return x * lax.rsqrt(jnp.mean(x * x, axis=-1, keepdims=True) + EPS) * g


def _const_spec(shape):
    nd = len(shape)
    return pl.BlockSpec(shape, lambda *_: (0,) * nd, pipeline_mode=pl.Buffered(1))


def _params(n_grid):
    return pltpu.CompilerParams(dimension_semantics=("arbitrary",) * n_grid,
                                vmem_limit_bytes=VMEM_LIMIT_BYTES)


def _dot(a, b):
    return jnp.dot(a, b, preferred_element_type=F32)


def _dot_nt(a, b):
    return lax.dot_general(a, b, (((1,), (1,)), ((), ())), preferred_element_type=F32)


def _dot_tn(a, b):
    return lax.dot_general(a, b, (((0,), (0,)), ((), ())), preferred_element_type=F32)


def _split3(v):
    h1 = v.astype(BF16)
    r1 = v - h1.astype(F32)
    h2 = r1.astype(BF16)
    r2 = r1 - h2.astype(F32)
    return h1, h2, r2.astype(BF16)


def _exact_dot(sel, v):
    h1, h2, h3 = _split3(v)
    return _dot(sel, h1) + _dot(sel, h2) + _dot(sel, h3)


def _expand(v, e2_ref):
    hi = v.astype(BF16)
    lo = (v - hi.astype(F32)).astype(BF16)
    return _dot(jnp.concatenate([hi, lo], axis=1), e2_ref[...])


def _inproj_kernel(x_ref, g_ref, wq_ref, wkv_ref, wz_ref, wxbc_ref, wdt_ref, wgate_ref,
                   qkv_ref, z_ref, xbc_ref, dt_ref, gate_ref):
    u = _rms(x_ref[...], g_ref[...]).astype(BF16)

    def proj(w_ref, o_ref, post):
        n = w_ref.shape[1]
        for c0 in range(0, n, PROJ_COLS):
            c1 = min(c0 + PROJ_COLS, n)
            o_ref[:, c0:c1] = post(_dot(u, w_ref[:, c0:c1])).astype(o_ref.dtype)

    ident = lambda t: t
    for c0 in range(0, ATT_WIDTH, PROJ_COLS):
        qkv_ref[:, c0:c0 + PROJ_COLS] = (_dot(u, wq_ref[:, c0:c0 + PROJ_COLS]) * Q_SCALE).astype(BF16)
    qkv_ref[:, ATT_WIDTH:] = _dot(u, wkv_ref[...]).astype(BF16)
    proj(wz_ref, z_ref, lambda t: t * jax.nn.sigmoid(t))
    proj(wxbc_ref, xbc_ref, ident)
    proj(wdt_ref, dt_ref, ident)
    proj(wgate_ref, gate_ref, jax.nn.sigmoid)


def _inproj(x2d, norm_mix, wq, wkv, wz, wxbc, wdt, wgate):
    t = x2d.shape[0]
    row = lambda n: pl.BlockSpec((ROW_TILE, n), lambda i: (i, 0))
    widths = (wq.shape[1] + wkv.shape[1], wz.shape[1], wxbc.shape[1], wdt.shape[1], wgate.shape[1])
    dtypes = (BF16, BF16, BF16, F32, BF16)
    return pl.pallas_call(
        _inproj_kernel,
        grid=(t // ROW_TILE,),
        in_specs=[row(D_MODEL), _const_spec((1, D_MODEL))]
        + [_const_spec(w.shape) for w in (wq, wkv, wz, wxbc, wdt, wgate)],
        out_specs=[row(n) for n in widths],
        out_shape=[jax.ShapeDtypeStruct((t, n), dt) for n, dt in zip(widths, dtypes)],
        compiler_params=_params(1),
        name="inproj",
    )(x2d, norm_mix, wq, wkv, wz, wxbc, wdt, wgate)


def _attn_kernel(seq_len, q_ref, kp_ref, kc_ref, kn_ref, vp_ref, vc_ref, vn_ref, half_ref, slope_ref,
                 sink_ref, wbr_ref, gate_ref, out_ref, ot_scr, s_scr):
    i = pl.program_id(1)
    tk = ATT_TQ + 2 * WINDOW
    n_qt = ATT_TQ // LANES
    kb = LANES + 2 * WINDOW
    j = lax.broadcasted_iota(jnp.int32, (kb, LANES), 0)
    r = lax.broadcasted_iota(jnp.int32, (kb, LANES), 1)
    dist = jnp.abs(r + WINDOW - j)
    mdist = []
    for c in range(n_qt):
        kpos = i * ATT_TQ - WINDOW + c * LANES + j
        valid = (dist <= WINDOW) & (kpos >= 0) & (kpos < seq_len)
        mdist.append(jnp.where(valid, dist.astype(F32), jnp.inf))
    k = jnp.concatenate([kp_ref[0], kc_ref[0], kn_ref[0]], axis=0)
    v = jnp.concatenate([vp_ref[0], vc_ref[0], vn_ref[0]], axis=0)
    k_halves, v2t = [], []
    for pair in range(ATT_KV_HEADS // 2):
        k2 = k[:, pair * LANES:(pair + 1) * LANES]
        k_halves.append((k2 * half_ref[0:1, :], k2 * half_ref[1:2, :]))
        v2t.append(v[:, pair * LANES:(pair + 1) * LANES].T)

    def logits(pos):
        tile, half = pos // 2, pos % 2
        qt = q_ref[0, :, tile * LANES:(tile + 1) * LANES]
        s = _dot_nt(k_halves[tile // ATT_REP][half], qt)
        for c in range(n_qt):
            s_scr[pos, c] = (s[c * LANES:c * LANES + kb, c * LANES:(c + 1) * LANES]
                             - slope_ref[pos] * mdist[c])

    def attend(pos):
        tile, half = pos // 2, pos % 2
        sink = sink_ref[pos]
        cols, denoms = [], []
        for c in range(n_qt):
            s = s_scr[pos, c]
            m = jnp.maximum(jnp.max(s, axis=0, keepdims=True), sink)
            p = jnp.exp2(s - m)
            denoms.append(jnp.sum(p, axis=0, keepdims=True) + jnp.exp2(sink - m))
            above, below = c * LANES, tk - kb - c * LANES
            cols.append(jnp.concatenate(
                ([jnp.zeros((above, LANES), BF16)] if above else []) + [p.astype(BF16)]
                + ([jnp.zeros((below, LANES), BF16)] if below else []), axis=0))
        vt = v2t[tile // ATT_REP][half * HEAD_DIM:(half + 1) * HEAD_DIM]
        ot = _dot(vt, jnp.concatenate(cols, axis=1)) / jnp.concatenate(denoms, axis=1)
        ot_scr[pos * HEAD_DIM:(pos + 1) * HEAD_DIM, :] = ot.astype(BF16)

    for pos in range(ATT_HEADS + ATT_SKEW):
        if pos < ATT_HEADS:
            logits(pos)
        if pos >= ATT_SKEW:
            attend(pos - ATT_SKEW)
    y = _dot_tn(ot_scr[...], wbr_ref[...])
    out_ref[0] = (gate_ref[0].astype(F32) * y).astype(out_ref.dtype)


_ATT_HEAD_ORDER = [ATT_REP * (2 * (p // (2 * ATT_REP)) + p % 2) + (p // 2) % ATT_REP for p in range(ATT_HEADS)]


def _attention(qkv, gate, slopes, sink, w_br_attn, b, seq_len):
    nq = seq_len // ATT_TQ
    halo_per_q = ATT_TQ // WINDOW
    n_halo = seq_len // WINDOW
    k_col, v_col = ATT_WIDTH // KV_WIDTH, ATT_WIDTH // KV_WIDTH + 1
    prev = lambda col: pl.BlockSpec(
        (1, WINDOW, KV_WIDTH), lambda bi, i: (bi, jnp.maximum(i * halo_per_q - 1, 0), col))
    cur = lambda col: pl.BlockSpec((1, ATT_TQ, KV_WIDTH), lambda bi, i: (bi, i, col))
    nxt = lambda col: pl.BlockSpec(
        (1, WINDOW, KV_WIDTH), lambda bi, i: (bi, jnp.minimum((i + 1) * halo_per_q, n_halo - 1), col))
    smem = pl.BlockSpec(memory_space=pltpu.SMEM)
    lane_half = np.arange(LANES) // HEAD_DIM
    half_masks = jnp.asarray(np.stack([lane_half == 0, lane_half == 1]), BF16)
    return pl.pallas_call(
        functools.partial(_attn_kernel, seq_len),
        grid=(b, nq),
        in_specs=[pl.BlockSpec((1, ATT_TQ, ATT_WIDTH), lambda bi, i: (bi, i, 0)),
                  prev(k_col), cur(k_col), nxt(k_col), prev(v_col), cur(v_col), nxt(v_col),
                  _const_spec(half_masks.shape), smem, smem, _const_spec(w_br_attn.shape),
                  pl.BlockSpec((1, ATT_TQ, D_MODEL), lambda bi, i: (bi, i, 0))],
        out_specs=pl.BlockSpec((1, ATT_TQ, D_MODEL), lambda bi, i: (bi, i, 0)),
        out_shape=jax.ShapeDtypeStruct((b, seq_len, D_MODEL), BF16),
        scratch_shapes=[pltpu.VMEM((ATT_WIDTH, ATT_TQ), BF16),
                        pltpu.VMEM((ATT_HEADS, ATT_TQ // LANES, LANES + 2 * WINDOW, LANES), F32)],
        compiler_params=_params(2),
        name="window_attn",
    )(qkv, qkv, qkv, qkv, qkv, qkv, qkv, half_masks, slopes, sink, w_br_attn, gate)


CONV_HALO = CONV_WIDTH // 2


def _head_tile(src, first_head, t):
    lane = lax.broadcasted_iota(jnp.int32, src.shape, 1)
    return jnp.take_along_axis(src, first_head + 2 * t + (lane >= SSM_HEAD_DIM).astype(jnp.int32), axis=1)


def _dt_terms(dt_ref, dtb_ref, a_ref, tril_ref, triu_ref):
    dt = jax.nn.softplus(dt_ref[0] + dtb_ref[...])
    dta = dt * a_ref[...]
    h1, h2, h3 = _split3(dta)
    pre = _dot(tril_ref[...], h1) + _dot(tril_ref[...], h2) + _dot(tril_ref[...], h3)
    suf = _dot(triu_ref[...], h1) + _dot(triu_ref[...], h2) + _dot(triu_ref[...], h3)
    lane = lax.broadcasted_iota(jnp.int32, dt.shape, 1)
    cum = jnp.where(lane < SSM_HEADS, pre, suf)
    return dt, cum


def _ssd_fwd_kernel(xp_ref, xc_ref, xn_ref, shift_ref, cw_ref, cb_ref, dt_ref, dtb_ref, a_ref, tril_ref,
                    triu_ref, xconv_ref, sin_ref, tcol_ref, trow_ref, state_scr):
    c = pl.program_id(1)

    @pl.when(c == 0)
    def _():
        state_scr[...] = jnp.zeros_like(state_scr)

    dt, cum = _dt_terms(dt_ref, dtb_ref, a_ref, tril_ref, triu_ref)

    taps_off = [kk for kk in range(CONV_WIDTH) if kk != CONV_HALO]
    for c0 in range(0, CONV_DIM, CONV_COLS):
        cols = slice(c0, c0 + CONV_COLS)
        cur = xc_ref[0, :, cols]
        ext = jnp.concatenate([xp_ref[0, :, cols], cur, xn_ref[0, :, cols]], axis=0)
        taps = jnp.concatenate([ext * cw_ref[kk:kk + 1, cols].astype(BF16) for kk in taps_off]
                               + [cur * cw_ref[CONV_HALO:CONV_HALO + 1, cols].astype(BF16)], axis=0)
        acc = _dot(shift_ref[0], taps) + cb_ref[:, cols]
        xconv_ref[0, :, cols] = (acc * jax.nn.sigmoid(acc)).astype(BF16)

    total = cum[CHUNK - 1:CHUNK, :]
    first = cum[0:1, :]
    lane64 = lax.broadcasted_iota(jnp.int32, dt.shape, 1)
    fwd_lane = lane64 < SSM_HEADS

    pad = jnp.zeros((CHUNK, LANES - 2 * SSM_HEADS), F32)
    wide = lambda t: jnp.concatenate([t, pad[:t.shape[0]]], axis=1)
    cum2 = cum * LOG2E
    tcol_ref[0, 0, 0:CHUNK] = wide(cum2)
    tcol_ref[0, 0, CHUNK:2 * CHUNK] = wide(jnp.exp(cum))
    tcol_ref[0, 0, 2 * CHUNK:3 * CHUNK] = wide(jnp.exp(jnp.where(fwd_lane, 0.0, first - cum)) * dt)
    tcol_ref[0, 0, 3 * CHUNK:] = wide(jnp.broadcast_to(jnp.exp(first), (8, 2 * SSM_HEADS)))
    row_t = wide(cum2 - jnp.log2(dt)).T
    dt_t = wide(dt).T
    diag_t = jnp.log2(dt_t[0:SSM_HEADS] + dt_t[SSM_HEADS:2 * SSM_HEADS])
    full = jnp.concatenate([row_t[0:2 * SSM_HEADS], diag_t,
                            jnp.zeros((LANES - 3 * SSM_HEADS, CHUNK), F32)], axis=0)
    swapped = pltpu.roll(full, CHUNK // 2, axis=1)
    next_row = lambda t: pltpu.roll(t, LANES - 1, axis=0)
    low = lax.broadcasted_iota(jnp.int32, full.shape, 1) < CHUNK // 2
    trow_ref[0, 0, 0] = jnp.where(low, full, next_row(swapped))
    trow_ref[0, 0, 1] = jnp.where(low, swapped, next_row(full))

    w = jnp.exp(jnp.where(fwd_lane, total - cum, 0.0)) * dt
    w_wide = wide(w)
    cdec_wide = wide(jnp.broadcast_to(jnp.exp(total), (8, 2 * SSM_HEADS)))
    tiles_per_group = GROUP_WIDTH // LANES
    for g in range(SSM_GROUPS):
        lo, hi = g * GROUP_WIDTH, (g + 1) * GROUP_WIDTH
        tiles = range(g * tiles_per_group, (g + 1) * tiles_per_group)
        wexp = jnp.concatenate([_head_tile(w_wide, 0, t) for t in tiles], axis=1)
        cdec = jnp.concatenate([_head_tile(cdec_wide, 0, t)[0:1, :] for t in tiles], axis=1)
        xw = (xconv_ref[0, :, lo:hi].astype(F32) * wexp).astype(BF16)
        bg = xconv_ref[0, :, SSM_INNER + g * SSM_STATE:SSM_INNER + (g + 1) * SSM_STATE]
        st = state_scr[g]
        sin_ref[0, 0, g] = st.astype(BF16)
        state_scr[g] = st * cdec + _dot_tn(bg, xw)


def _ssd_main_kernel(nc, xc_ref, tcol_ref, trow_ref, z_ref, sin_ref, attn_ref, gate_ref, xres_ref,
                     half_ref, e2f_ref, dskip_ref, snorm_ref,
                     wbr_ref, wout_ref, out_ref, y_scr, state_scr, exp_scr, yn_scr, cb_scr, m_scr):
    c = pl.program_id(1)

    @pl.when(c == 0)
    def _():
        state_scr[...] = jnp.zeros_like(state_scr)
        yn_scr[...] = jnp.zeros_like(yn_scr)

    def project():
        y_ssm = _dot(yn_scr[...], wbr_ref[...])
        merged = attn_ref[0].astype(F32) + gate_ref[0].astype(F32) * y_ssm
        out_ref[0] = xres_ref[0] + _dot(merged.astype(BF16), wout_ref[...])

    pl.when(c == nc)(project)

    @pl.when(c < nc)
    def _():
        _ssd_chunk(project, xc_ref, tcol_ref, trow_ref, z_ref, sin_ref, half_ref, e2f_ref, dskip_ref,
                   snorm_ref, y_scr, state_scr, exp_scr, yn_scr, cb_scr, m_scr)


def _ssd_chunk(project, xc_ref, tcol_ref, trow_ref, z_ref, sin_ref, half_ref, e2f_ref, dskip_ref,
               snorm_ref, y_scr, state_scr, exp_scr, yn_scr, cb_scr, m_scr):
    half = CHUNK // 2
    low = lax.broadcasted_iota(jnp.int32, (CHUNK, LANES), 1) < half

    def b_c(g):
        return (xc_ref[0, :, SSM_INNER + g * SSM_STATE:SSM_INNER + (g + 1) * SSM_STATE],
                xc_ref[0, :, SSM_INNER + SSM_GN + g * SSM_STATE:SSM_INNER + SSM_GN + (g + 1) * SSM_STATE])

    for g in range(SSM_GROUPS):
        bg, cg = b_c(g)
        cb = _dot_nt(cg, bg)
        cb_sw = pltpu.roll(cb, half, axis=1)
        cb_scr[g, 0] = jnp.where(low, cb, cb_sw)
        cb_scr[g, 1] = jnp.where(low, cb_sw, cb)

    project()
    exp_scr[0] = _expand(tcol_ref[0, 0, CHUNK:2 * CHUNK], e2f_ref)

    head_tile = _head_tile

    def expand_terms():
        ecum = tcol_ref[0, 0, CHUNK:2 * CHUNK]
        wb = tcol_ref[0, 0, 2 * CHUNK:3 * CHUNK]
        cdec8 = tcol_ref[0, 0, 3 * CHUNK:]
        cdec_tiles = []
        for t in range(SSM_INNER // LANES):
            cols = slice(t * LANES, (t + 1) * LANES)
            exp_scr[1, :, cols] = head_tile(ecum, SSM_HEADS, t)
            exp_scr[2, :, cols] = head_tile(wb, SSM_HEADS, t)
            cdec_tiles.append(head_tile(cdec8, SSM_HEADS, t)[0:1, :])
        return jnp.concatenate(cdec_tiles, axis=1)

    cum2 = tcol_ref[0, 0, 0:CHUNK]
    src = lax.broadcasted_iota(jnp.int32, (half, LANES), 1) % half
    row = lax.broadcasted_iota(jnp.int32, (half, LANES), 0)
    lower, upper = row > src, row < src
    top, bot = slice(0, half), slice(half, CHUNK)

    def decay_matrices(g):
        for pr in range(SSM_REP // 2):
            h = g * SSM_REP + 2 * pr
            hb = SSM_HEADS + h
            col_f = jnp.take_along_axis(cum2, jnp.where(low, h, h + 1), axis=1)
            col_b = jnp.take_along_axis(cum2, jnp.where(low, hb, hb + 1), axis=1)
            row_f = [trow_ref[0, 0, j, h:h + 1, :] for j in range(2)]
            row_b = [trow_ref[0, 0, j, hb:hb + 1, :] for j in range(2)]
            diag = [trow_ref[0, 0, j, 2 * SSM_HEADS + h:2 * SSM_HEADS + h + 1, :] for j in range(2)]
            mixed = lambda rows, j: jnp.where(lower, col_f[rows] - row_f[j],
                                              jnp.where(upper, col_b[rows] - row_b[j], diag[j]))
            args = [[mixed(top, 0), col_b[top] - row_b[1]],
                    [col_f[bot] - row_f[0], mixed(bot, 1)]]
            for j in range(2):
                for i, rows in enumerate((top, bot)):
                    m_scr[h // 2, rows, j * LANES:(j + 1) * LANES] = (
                        cb_scr[g, j, rows, :] * jnp.exp2(args[i][j])).astype(BF16)

    def inter(g):
        lo, hi = g * GROUP_WIDTH, (g + 1) * GROUP_WIDTH
        bg, cg = b_c(g)
        st = state_scr[g]
        xg = xc_ref[0, :, lo:hi].astype(F32)
        y_scr[:, lo:hi] = (_dot(cg, sin_ref[0, 0, g]) * exp_scr[0, :, lo:hi]
                           + _dot(cg, st.astype(BF16)) * exp_scr[1, :, lo:hi] + dskip_ref[:, lo:hi] * xg)
        xw = (xg * exp_scr[2, :, lo:hi]).astype(BF16)
        state_scr[g] = st * cdec[:, lo:hi] + _dot_tn(bg, xw)

    def intra(g):
        m0, m1 = half_ref[0:1, :], half_ref[1:2, :]
        for pr in range(SSM_REP // 2):
            h = g * SSM_REP + 2 * pr
            cols = slice(h * SSM_HEAD_DIM, h * SSM_HEAD_DIM + LANES)
            xa = xc_ref[0, 0:half, cols]
            xb = xc_ref[0, half:, cols]
            xdiag = jnp.concatenate([xa * m0, xa * m1, xb * m0, xb * m1], axis=0)
            y_scr[:, cols] = (y_scr[:, cols] + _dot(m_scr[h // 2], xdiag)) * z_ref[0, :, cols].astype(F32)

    for g in range(SSM_GROUPS):
        decay_matrices(g)
    cdec = expand_terms()
    for g in range(SSM_GROUPS):
        inter(g)
    for g in range(SSM_GROUPS):
        intra(g)

    yn_scr[...] = _rms(y_scr[...], snorm_ref[...]).astype(BF16)


def _ssd_constants():
    idx = np.arange(CHUNK)
    tril = (idx[:, None] >= idx[None, :]).astype(np.float32)
    heads = np.arange(2 * SSM_HEADS)
    chan_head = np.arange(SSM_INNER) // SSM_HEAD_DIM
    e_f = (heads[:, None] == chan_head[None, :]).astype(np.float32)
    stack = lambda e: jnp.asarray(np.concatenate([e, e], axis=0), BF16)
    widen = lambda e: np.concatenate([e, np.zeros((LANES - e.shape[0], e.shape[1]), np.float32)], axis=0)
    ext_rows = CHUNK + 2 * BF16_SUBLANES
    taps_off = [kk for kk in range(CONV_WIDTH) if kk != CONV_HALO]
    shift = np.zeros((4, CHUNK, len(taps_off) * ext_rows + CHUNK), np.float32)
    shift[:, idx, len(taps_off) * ext_rows + idx] = 1.0
    for n, kk in enumerate(taps_off):
        shift[:, idx, n * ext_rows + BF16_SUBLANES + idx + kk - CONV_HALO] = 1.0
        lo_halo = slice(n * ext_rows, n * ext_rows + BF16_SUBLANES)
        hi_halo = slice(n * ext_rows + BF16_SUBLANES + CHUNK, (n + 1) * ext_rows)
        shift[0, :, lo_halo] = 0.0
        shift[1, :, lo_halo] = 0.0
        shift[0, :, hi_halo] = 0.0
        shift[2, :, hi_halo] = 0.0
    lane_half = np.arange(LANES) // SSM_HEAD_DIM
    return dict(tril=jnp.asarray(tril, BF16), triu=jnp.asarray(tril.T, BF16),
                e2f_wide=stack(widen(e_f)), shift=jnp.asarray(shift, BF16),
                half_masks=jnp.asarray(np.stack([lane_half == 0, lane_half == 1]), BF16))


TCOL_ROWS = 3 * CHUNK + 8


def _ssd_fwd(xbc, dt_raw, conv_w, conv_b, dt_bias, a_neg, consts, b, seq_len):
    tril, triu, shift = consts["tril"], consts["triu"], consts["shift"]
    nc = seq_len // CHUNK
    halo_blocks = CHUNK // BF16_SUBLANES
    n_halo = seq_len // BF16_SUBLANES
    shift_spec = pl.BlockSpec(
        (1,) + shift.shape[1:],
        lambda bi, c: (2 * (c > 0).astype(jnp.int32) + (c < nc - 1).astype(jnp.int32), 0, 0))
    return pl.pallas_call(
        _ssd_fwd_kernel,
        grid=(b, nc),
        in_specs=[pl.BlockSpec((1, BF16_SUBLANES, CONV_DIM),
                               lambda bi, c: (bi, jnp.maximum(c * halo_blocks - 1, 0), 0)),
                  pl.BlockSpec((1, CHUNK, CONV_DIM), lambda bi, c: (bi, c, 0)),
                  pl.BlockSpec((1, BF16_SUBLANES, CONV_DIM),
                               lambda bi, c: (bi, jnp.minimum((c + 1) * halo_blocks, n_halo - 1), 0)),
                  shift_spec, _const_spec(conv_w.shape), _const_spec(conv_b.shape),
                  pl.BlockSpec((1, CHUNK, 2 * SSM_HEADS), lambda bi, c: (bi, c, 0)),
                  _const_spec(dt_bias.shape), _const_spec(a_neg.shape),
                  _const_spec(tril.shape), _const_spec(triu.shape)],
        out_specs=[pl.BlockSpec((1, CHUNK, CONV_DIM), lambda bi, c: (bi, c, 0)),
                   pl.BlockSpec((1, 1, SSM_GROUPS, SSM_STATE, GROUP_WIDTH), lambda bi, c: (bi, c, 0, 0, 0)),
                   pl.BlockSpec((1, 1, TCOL_ROWS, LANES), lambda bi, c: (bi, c, 0, 0)),
                   pl.BlockSpec((1, 1, 2, LANES, CHUNK), lambda bi, c: (bi, c, 0, 0, 0))],
        out_shape=[jax.ShapeDtypeStruct((b, seq_len, CONV_DIM), BF16),
                   jax.ShapeDtypeStruct((b, nc, SSM_GROUPS, SSM_STATE, GROUP_WIDTH), BF16),
                   jax.ShapeDtypeStruct((b, nc, TCOL_ROWS, LANES), F32),
                   jax.ShapeDtypeStruct((b, nc, 2, LANES, CHUNK), F32)],
        scratch_shapes=[pltpu.VMEM((SSM_GROUPS, SSM_STATE, GROUP_WIDTH), F32)],
        compiler_params=_params(2),
        name="ssd_fwd",
    )(xbc, xbc, xbc, shift, conv_w, conv_b, dt_raw, dt_bias, a_neg, tril, triu)


def _ssd_main(xconv, tcol, trow, z, s_in, attn_part, gate, x3d, consts, d_skip_exp, ssm_norm, w_br_ssm, w_out,
              b, seq_len):
    nc = seq_len // CHUNK
    ssd_chunk = lambda c: nc - 1 - jnp.minimum(c, nc - 1)
    proj_chunk = lambda c: nc - 1 - jnp.maximum(c - 1, 0)
    rev = lambda width: pl.BlockSpec((1, CHUNK, width), lambda bi, c: (bi, ssd_chunk(c), 0))
    rev_chunk = lambda shape: pl.BlockSpec((1, 1) + shape, lambda bi, c: (bi, ssd_chunk(c)) + (0,) * len(shape))
    late = lambda col=0: pl.BlockSpec((1, CHUNK, D_MODEL), lambda bi, c: (bi, proj_chunk(c), col))
    small = (consts["half_masks"], consts["e2f_wide"], d_skip_exp, ssm_norm, w_br_ssm, w_out)
    return pl.pallas_call(
        functools.partial(_ssd_main_kernel, nc),
        grid=(b, nc + 1),
        in_specs=[rev(CONV_DIM), rev_chunk((TCOL_ROWS, LANES)), rev_chunk((2, LANES, CHUNK)), rev(SSM_INNER),
                  rev_chunk((SSM_GROUPS, SSM_STATE, GROUP_WIDTH)), late(), late(1), late()]
        + [_const_spec(a.shape) for a in small],
        out_specs=late(),
        out_shape=jax.ShapeDtypeStruct((b, seq_len, D_MODEL), F32),
        scratch_shapes=[pltpu.VMEM((CHUNK, SSM_INNER), F32),
                        pltpu.VMEM((SSM_GROUPS, SSM_STATE, GROUP_WIDTH), F32),
                        pltpu.VMEM((3, CHUNK, SSM_INNER), F32),
                        pltpu.VMEM((CHUNK, SSM_INNER), BF16),
                        pltpu.VMEM((SSM_GROUPS, 2, CHUNK, LANES), F32),
                        pltpu.VMEM((SSM_HEADS // 2, CHUNK, 2 * LANES), BF16)],
        compiler_params=_params(2),
        name="ssd_main",
    )(xconv, tcol, trow, z, s_in, attn_part, gate, x3d, *small)


def _memkv_kernel(mem_ref, g_ref, wkv_ref, kv_ref):
    mn = _rms(mem_ref[0], g_ref[...]).astype(BF16)
    for c0 in range(0, 2 * D_MODEL, PROJ_COLS):
        kv_ref[0, :, c0:c0 + PROJ_COLS] = _dot(mn, wkv_ref[:, c0:c0 + PROJ_COLS]).astype(BF16)


def _memkv(mem, norm_mem, w_kv):
    b, m, _ = mem.shape
    return pl.pallas_call(
        _memkv_kernel,
        grid=(b,),
        in_specs=[pl.BlockSpec((1, m, D_MODEL), lambda bi: (bi, 0, 0)),
                  _const_spec(norm_mem.shape), _const_spec(w_kv.shape)],
        out_specs=pl.BlockSpec((1, m, 2 * D_MODEL), lambda bi: (bi, 0, 0)),
        out_shape=jax.ShapeDtypeStruct((b, m, 2 * D_MODEL), BF16),
        compiler_params=_params(1),
        name="mem_kv",
    )(mem, norm_mem, w_kv)


def _tail_kernel(x_ref, kv_ref, ncross_ref, wq_ref, wo_ref, nffn_ref, wg_ref, wu_ref, wd_ref, nfin_ref,
                 out_ref, o_scr, h_scr, q_scr, s_scr):
    x = x_ref[0]
    u = _rms(x, ncross_ref[...]).astype(BF16)
    for c0 in range(0, D_MODEL, PROJ_COLS):
        q_scr[:, c0:c0 + PROJ_COLS] = (_dot(u, wq_ref[:, c0:c0 + PROJ_COLS]) * XQ_SCALE).astype(BF16)

    def logits(h):
        lo, hi = h * X_HEAD_DIM, (h + 1) * X_HEAD_DIM
        s_scr[h] = _dot_nt(q_scr[:, lo:hi], kv_ref[0, :, lo:hi])

    def attend(h):
        lo, hi = h * X_HEAD_DIM, (h + 1) * X_HEAD_DIM
        s = s_scr[h]
        p = jnp.exp2(s - jnp.max(s, axis=-1, keepdims=True))
        denom = jnp.sum(p, axis=-1, keepdims=True)
        o = _dot(p.astype(BF16), kv_ref[0, :, D_MODEL + lo:D_MODEL + hi]) / denom
        o_scr[:, lo:hi] = o.astype(BF16)

    logits(0)
    for h in range(X_HEADS):
        if h + 1 < X_HEADS:
            logits(h + 1)
        attend(h)
    x = x + _dot(o_scr[...], wo_ref[...])
    u = _rms(x, nffn_ref[...]).astype(BF16)
    for c0 in range(0, FFN_HIDDEN, PROJ_COLS):
        c1 = min(c0 + PROJ_COLS, FFN_HIDDEN)
        gte = _dot(u, wg_ref[:, c0:c1])
        up = _dot(u, wu_ref[:, c0:c1])
        h_scr[:, c0:c1] = (gte * jax.nn.sigmoid(gte) * up).astype(BF16)
    x = x + _dot(h_scr[...], wd_ref[...])
    out_ref[0] = _rms(x, nfin_ref[...])


def _tail(x3d, kv, norm_cross, w_q, w_o, norm_ffn, w_gate, w_up, w_down, norm_final):
    b, seq_len, _ = x3d.shape
    m = kv.shape[1]
    consts = (norm_cross, w_q, w_o, norm_ffn, w_gate, w_up, w_down, norm_final)
    return pl.pallas_call(
        _tail_kernel,
        grid=(b, seq_len // ROW_TILE),
        in_specs=[pl.BlockSpec((1, ROW_TILE, D_MODEL), lambda bi, i: (bi, i, 0)),
                  pl.BlockSpec((1, m, 2 * D_MODEL), lambda bi, i: (bi, 0, 0))]
        + [_const_spec(a.shape) for a in consts],
        out_specs=pl.BlockSpec((1, ROW_TILE, D_MODEL), lambda bi, i: (bi, i, 0)),
        out_shape=jax.ShapeDtypeStruct((b, seq_len, D_MODEL), F32),
        scratch_shapes=[pltpu.VMEM((ROW_TILE, D_MODEL), BF16), pltpu.VMEM((ROW_TILE, FFN_HIDDEN), BF16),
                        pltpu.VMEM((ROW_TILE, D_MODEL), BF16), pltpu.VMEM((X_HEADS, ROW_TILE, m), F32)],
        compiler_params=_params(2),
        name="cross_ffn",
    )(x3d, kv, *consts)


def _prepare(norm_mix, w_in, conv_w, conv_b, attn_sink, a_log, dt_bias, d_skip, ssm_norm, w_br_attn,
             w_br_ssm, w_out, norm_cross, norm_mem, w_q_cross, w_kv_cross, w_o_cross, norm_ffn, w_gate_up,
             w_down, norm_final):
    row = lambda v: v.reshape(1, -1).astype(F32)
    w = w_in[0]
    cuts = np.cumsum([0, ATT_WIDTH, 2 * KV_WIDTH, SSM_INNER, CONV_DIM, 2 * SSM_HEADS, 2 * D_MODEL])
    w = w.astype(BF16)
    wq, wkv, wz, wxbc, wdt, wgate = (w[:, int(s):int(e)] for s, e in zip(cuts[:-1], cuts[1:]))
    grid5 = (ATT_KV_HEADS // 2, 2, ATT_REP)
    reorder = lambda t, axis: jnp.swapaxes(
        t.reshape(t.shape[:axis] + grid5 + (HEAD_DIM,) + t.shape[axis + 1:]), axis + 1, axis + 2).reshape(t.shape)
    order = np.asarray(_ATT_HEAD_ORDER)
    slopes = jnp.exp2(-8.0 * jnp.arange(1, ATT_HEADS + 1, dtype=F32) / ATT_HEADS)
    return dict(
        norm_mix=row(norm_mix[0]), wq=reorder(wq, 1), wkv=wkv, wz=wz, wxbc=wxbc, wdt=wdt, wgate=wgate,
        conv_w=conv_w[0].astype(F32), conv_b=row(conv_b[0]),
        slopes=slopes[order] * LOG2E, sink=attn_sink[0].astype(F32)[order] * LOG2E,
        a_neg=row(-jnp.exp(a_log[0].astype(F32))), dt_bias=row(dt_bias[0]),
        d_skip=row(jnp.repeat(d_skip[0].astype(F32), SSM_HEAD_DIM)), ssm_norm=row(ssm_norm[0]),
        w_br_attn=reorder(w_br_attn[0].astype(BF16), 0), w_br_ssm=w_br_ssm[0].astype(BF16),
        w_out=w_out[0].astype(BF16),
        norm_cross=row(norm_cross[0]), norm_mem=row(norm_mem[0]),
        w_q=w_q_cross[0].astype(BF16), w_kv=w_kv_cross[0].astype(BF16), w_o=w_o_cross[0].astype(BF16),
        norm_ffn=row(norm_ffn[0]), w_gate=w_gate_up[0][:, :FFN_HIDDEN].astype(BF16),
        w_up=w_gate_up[0][:, FFN_HIDDEN:].astype(BF16), w_down=w_down[0].astype(BF16),
        norm_final=row(norm_final), consts=_ssd_constants())


def _trunk(x, mem, p):
    b, seq_len, _ = x.shape
    assert seq_len % ROW_TILE == 0 and seq_len % ATT_TQ == 0 and seq_len % CHUNK == 0
    qkv, z, xbc, dt_raw, gate = _inproj(x.reshape(b * seq_len, D_MODEL), p["norm_mix"], p["wq"], p["wkv"],
                                        p["wz"], p["wxbc"], p["wdt"], p["wgate"])
    as3d = lambda t: t.reshape(b, seq_len, t.shape[-1])
    qkv, z, xbc, dt_raw, gate = as3d(qkv), as3d(z), as3d(xbc), as3d(dt_raw), as3d(gate)
    attn_part = _attention(qkv, gate, p["slopes"], p["sink"], p["w_br_attn"], b, seq_len)
    xconv, s_in, tcol, trow = _ssd_fwd(xbc, dt_raw, p["conv_w"], p["conv_b"], p["dt_bias"], p["a_neg"],
                                       p["consts"], b, seq_len)
    x1 = _ssd_main(xconv, tcol, trow, z, s_in, attn_part, gate, x, p["consts"], p["d_skip"], p["ssm_norm"],
                   p["w_br_ssm"], p["w_out"], b, seq_len)
    kv = _memkv(mem, p["norm_mem"], p["w_kv"])
    return _tail(x1, kv, p["norm_cross"], p["w_q"], p["w_o"], p["norm_ffn"], p["w_gate"], p["w_up"],
                 p["w_down"], p["norm_final"])


def kernel(x_prompt, x_sample, mem_prompt, mem_sample, norm_mix, w_in, conv_w, conv_b, attn_sink, a_log,
           dt_bias, d_skip, ssm_norm, w_br_attn, w_br_ssm, w_out, norm_cross, norm_mem, w_q_cross,
           w_kv_cross, w_o_cross, norm_ffn, w_gate_up, w_down, norm_final):
    p = _prepare(norm_mix, w_in, conv_w, conv_b, attn_sink, a_log, dt_bias, d_skip, ssm_norm, w_br_attn,
                 w_br_ssm, w_out, norm_cross, norm_mem, w_q_cross, w_kv_cross, w_o_cross, norm_ffn,
                 w_gate_up, w_down, norm_final)
    return (_trunk(x_prompt, mem_prompt, p), _trunk(x_sample, mem_sample, p))
```

```python
import functools

import numpy as np
import jax
import jax.numpy as jnp
from jax import lax
from jax.experimental import pallas as pl
from jax.experimental.pallas import tpu as pltpu

F32 = jnp.float32
BF16 = jnp.bfloat16

D_MODEL = 1024
EPS = 1e-6
ATT_HEADS = 16
ATT_KV_HEADS = 4
ATT_REP = ATT_HEADS // ATT_KV_HEADS
HEAD_DIM = 64
ATT_WIDTH = ATT_HEADS * HEAD_DIM
KV_WIDTH = ATT_KV_HEADS * HEAD_DIM
WINDOW = 128
SSM_INNER = 2 * D_MODEL
SSM_HEAD_DIM = 64
SSM_HEADS = SSM_INNER // SSM_HEAD_DIM
SSM_GROUPS = 4
SSM_REP = SSM_HEADS // SSM_GROUPS
SSM_STATE = 128
SSM_GN = SSM_GROUPS * SSM_STATE
CONV_WIDTH = 5
CONV_DIM = SSM_INNER + 2 * SSM_GN
CHUNK = 128
GROUP_WIDTH = SSM_REP * SSM_HEAD_DIM
X_HEADS = 4
X_HEAD_DIM = D_MODEL // X_HEADS
FFN_HIDDEN = -(-8 * D_MODEL // (3 * 256)) * 256

LANES = 128
BF16_SUBLANES = 16
VMEM_LIMIT_BYTES = 56 * 1024 * 1024

ROW_TILE = 512
ATT_TQ = 256
ATT_SKEW = 6
PROJ_COLS = 512
CONV_COLS = 512

LOG2E = 1.4426950408889634
Q_SCALE = HEAD_DIM ** -0.5 * LOG2E
XQ_SCALE = X_HEAD_DIM ** -0.5 * LOG2E


def _rms(x, g):
    return x * lax.rsqrt(jnp.mean(x * x, axis=-1, keepdims=True) + EPS) * g


def _const_spec(shape):
    nd = len(shape)
    return pl.BlockSpec(shape, lambda *_: (0,) * nd, pipeline_mode=pl.Buffered(1))


def _params(n_grid):
    return pltpu.CompilerParams(dimension_semantics=("arbitrary",) * n_grid,
                                vmem_limit_bytes=VMEM_LIMIT_BYTES)


def _dot(a, b):
    return jnp.dot(a, b, preferred_element_type=F32)


def _dot_nt(a, b):
    return lax.dot_general(a, b, (((1,), (1,)), ((), ())), preferred_element_type=F32)


def _dot_tn(a, b):
    return lax.dot_general(a, b, (((0,), (0,)), ((), ())), preferred_element_type=F32)


def _split3(v):
    h1 = v.astype(BF16)
    r1 = v - h1.astype(F32)
    h2 = r1.astype(BF16)
    r2 = r1 - h2.astype(F32)
    return h1, h2, r2.astype(BF16)


def _exact_dot(sel, v):
    h1, h2, h3 = _split3(v)
    return _dot(sel, h1) + _dot(sel, h2) + _dot(sel, h3)


def _expand(v, e2_ref):
    hi = v.astype(BF16)
    lo = (v - hi.astype(F32)).astype(BF16)
    return _dot(jnp.concatenate([hi, lo], axis=1), e2_ref[...])


def _inproj_kernel(x_ref, g_ref, wq_ref, wkv_ref, wz_ref, wxbc_ref, wdt_ref, wgate_ref,
                   qkv_ref, z_ref, xbc_ref, dt_ref, gate_ref):
    u = _rms(x_ref[...], g_ref[...]).astype(BF16)

    def proj(w_ref, o_ref, post):
        n = w_ref.shape[1]
        for c0 in range(0, n, PROJ_COLS):
            c1 = min(c0 + PROJ_COLS, n)
            o_ref[:, c0:c1] = post(_dot(u, w_ref[:, c0:c1])).astype(o_ref.dtype)

    ident = lambda t: t
    for c0 in range(0, ATT_WIDTH, PROJ_COLS):
        qkv_ref[:, c0:c0 + PROJ_COLS] = (_dot(u, wq_ref[:, c0:c0 + PROJ_COLS]) * Q_SCALE).astype(BF16)
    qkv_ref[:, ATT_WIDTH:] = _dot(u, wkv_ref[...]).astype(BF16)
    proj(wz_ref, z_ref, lambda t: t * jax.nn.sigmoid(t))
    proj(wxbc_ref, xbc_ref, ident)
    proj(wdt_ref, dt_ref, ident)
    proj(wgate_ref, gate_ref, jax.nn.sigmoid)


def _inproj(x2d, norm_mix, wq, wkv, wz, wxbc, wdt, wgate):
    t = x2d.shape[0]
    row = lambda n: pl.BlockSpec((ROW_TILE, n), lambda i: (i, 0))
    widths = (wq.shape[1] + wkv.shape[1], wz.shape[1], wxbc.shape[1], wdt.shape[1], wgate.shape[1])
    dtypes = (BF16, BF16, BF16, F32, BF16)
    return pl.pallas_call(
        _inproj_kernel,
        grid=(t // ROW_TILE,),
        in_specs=[row(D_MODEL), _const_spec((1, D_MODEL))]
        + [_const_spec(w.shape) for w in (wq, wkv, wz, wxbc, wdt, wgate)],
        out_specs=[row(n) for n in widths],
        out_shape=[jax.ShapeDtypeStruct((t, n), dt) for n, dt in zip(widths, dtypes)],
        compiler_params=_params(1),
        name="inproj",
    )(x2d, norm_mix, wq, wkv, wz, wxbc, wdt, wgate)


def _attn_kernel(seq_len, q_ref, kp_ref, kc_ref, kn_ref, vp_ref, vc_ref, vn_ref, half_ref, slope_ref,
                 sink_ref, wbr_ref, gate_ref, out_ref, ot_scr, s_scr):
    i = pl.program_id(1)
    tk = ATT_TQ + 2 * WINDOW
    n_qt = ATT_TQ // LANES
    kb = LANES + 2 * WINDOW
    j = lax.broadcasted_iota(jnp.int32, (kb, LANES), 0)
    r = lax.broadcasted_iota(jnp.int32, (kb, LANES), 1)
    dist = jnp.abs(r + WINDOW - j)
    mdist = []
    for c in range(n_qt):
        kpos = i * ATT_TQ - WINDOW + c * LANES + j
        valid = (dist <= WINDOW) & (kpos >= 0) & (kpos < seq_len)
        mdist.append(jnp.where(valid, dist.astype(F32), jnp.inf))
    k = jnp.concatenate([kp_ref[0], kc_ref[0], kn_ref[0]], axis=0)
    v = jnp.concatenate([vp_ref[0], vc_ref[0], vn_ref[0]], axis=0)
    k_halves, v2t = [], []
    for pair in range(ATT_KV_HEADS // 2):
        k2 = k[:, pair * LANES:(pair + 1) * LANES]
        k_halves.append((k2 * half_ref[0:1, :], k2 * half_ref[1:2, :]))
        v2t.append(v[:, pair * LANES:(pair + 1) * LANES].T)

    def logits(pos):
        tile, half = pos // 2, pos % 2
        qt = q_ref[0, :, tile * LANES:(tile + 1) * LANES]
        s = _dot_nt(k_halves[tile // ATT_REP][half], qt)
        for c in range(n_qt):
            s_scr[pos, c] = (s[c * LANES:c * LANES + kb, c * LANES:(c + 1) * LANES]
                             - slope_ref[pos] * mdist[c])

    def attend(pos):
        tile, half = pos // 2, pos % 2
        sink = sink_ref[pos]
        cols, denoms = [], []
        for c in range(n_qt):
            s = s_scr[pos, c]
            m = jnp.maximum(jnp.max(s, axis=0, keepdims=True), sink)
            p = jnp.exp2(s - m)
            denoms.append(jnp.sum(p, axis=0, keepdims=True) + jnp.exp2(sink - m))
            above, below = c * LANES, tk - kb - c * LANES
            cols.append(jnp.concatenate(
                ([jnp.zeros((above, LANES), BF16)] if above else []) + [p.astype(BF16)]
                + ([jnp.zeros((below, LANES), BF16)] if below else []), axis=0))
        vt = v2t[tile // ATT_REP][half * HEAD_DIM:(half + 1) * HEAD_DIM]
        ot = _dot(vt, jnp.concatenate(cols, axis=1)) / jnp.concatenate(denoms, axis=1)
        ot_scr[pos * HEAD_DIM:(pos + 1) * HEAD_DIM, :] = ot.astype(BF16)

    for pos in range(ATT_HEADS + ATT_SKEW):
        if pos < ATT_HEADS:
            logits(pos)
        if pos >= ATT_SKEW:
            attend(pos - ATT_SKEW)
    y = _dot_tn(ot_scr[...], wbr_ref[...])
    out_ref[0] = (gate_ref[0].astype(F32) * y).astype(out_ref.dtype)


_ATT_HEAD_ORDER = [ATT_REP * (2 * (p // (2 * ATT_REP)) + p % 2) + (p // 2) % ATT_REP for p in range(ATT_HEADS)]


def _attention(qkv, gate, slopes, sink, w_br_attn, b, seq_len):
    nq = seq_len // ATT_TQ
    halo_per_q = ATT_TQ // WINDOW
    n_halo = seq_len // WINDOW
    k_col, v_col = ATT_WIDTH // KV_WIDTH, ATT_WIDTH // KV_WIDTH + 1
    prev = lambda col: pl.BlockSpec(
        (1, WINDOW, KV_WIDTH), lambda bi, i: (bi, jnp.maximum(i * halo_per_q - 1, 0), col))
    cur = lambda col: pl.BlockSpec((1, ATT_TQ, KV_WIDTH), lambda bi, i: (bi, i, col))
    nxt = lambda col: pl.BlockSpec(
        (1, WINDOW, KV_WIDTH), lambda bi, i: (bi, jnp.minimum((i + 1) * halo_per_q, n_halo - 1), col))
    smem = pl.BlockSpec(memory_space=pltpu.SMEM)
    lane_half = np.arange(LANES) // HEAD_DIM
    half_masks = jnp.asarray(np.stack([lane_half == 0, lane_half == 1]), BF16)
    return pl.pallas_call(
        functools.partial(_attn_kernel, seq_len),
        grid=(b, nq),
        in_specs=[pl.BlockSpec((1, ATT_TQ, ATT_WIDTH), lambda bi, i: (bi, i, 0)),
                  prev(k_col), cur(k_col), nxt(k_col), prev(v_col), cur(v_col), nxt(v_col),
                  _const_spec(half_masks.shape), smem, smem, _const_spec(w_br_attn.shape),
                  pl.BlockSpec((1, ATT_TQ, D_MODEL), lambda bi, i: (bi, i, 0))],
        out_specs=pl.BlockSpec((1, ATT_TQ, D_MODEL), lambda bi, i: (bi, i, 0)),
        out_shape=jax.ShapeDtypeStruct((b, seq_len, D_MODEL), BF16),
        scratch_shapes=[pltpu.VMEM((ATT_WIDTH, ATT_TQ), BF16),
                        pltpu.VMEM((ATT_HEADS, ATT_TQ // LANES, LANES + 2 * WINDOW, LANES), F32)],
        compiler_params=_params(2),
        name="window_attn",
    )(qkv, qkv, qkv, qkv, qkv, qkv, qkv, half_masks, slopes, sink, w_br_attn, gate)


CONV_HALO = CONV_WIDTH // 2


def _head_tile(src, first_head, t):
    lane = lax.broadcasted_iota(jnp.int32, src.shape, 1)
    return jnp.take_along_axis(src, first_head + 2 * t + (lane >= SSM_HEAD_DIM).astype(jnp.int32), axis=1)


def _dt_terms(dt_ref, dtb_ref, a_ref, tril_ref, triu_ref):
    dt = jax.nn.softplus(dt_ref[0] + dtb_ref[...])
    dta = dt * a_ref[...]
    h1, h2, h3 = _split3(dta)
    pre = _dot(tril_ref[...], h1) + _dot(tril_ref[...], h2) + _dot(tril_ref[...], h3)
    suf = _dot(triu_ref[...], h1) + _dot(triu_ref[...], h2) + _dot(triu_ref[...], h3)
    lane = lax.broadcasted_iota(jnp.int32, dt.shape, 1)
    cum = jnp.where(lane < SSM_HEADS, pre, suf)
    return dt, cum


def _ssd_fwd_kernel(xc_ref, xn_ref, shift_ref, cw_ref, cb_ref, dt_ref, dtb_ref, a_ref, tril_ref,
                    triu_ref, xconv_ref, sin_ref, tcol_ref, trow_ref, state_scr, prev_scr):
    c = pl.program_id(1)

    @pl.when(c == 0)
    def _():
        state_scr[...] = jnp.zeros_like(state_scr)
        prev_scr[...] = jnp.zeros_like(prev_scr)

    dt, cum = _dt_terms(dt_ref, dtb_ref, a_ref, tril_ref, triu_ref)

    taps_off = [kk for kk in range(CONV_WIDTH) if kk != CONV_HALO]
    for c0 in range(0, CONV_DIM, CONV_COLS):
        cols = slice(c0, c0 + CONV_COLS)
        cur = xc_ref[0, :, cols]
        ext = jnp.concatenate([prev_scr[:, cols], cur, xn_ref[0, :, cols]], axis=0)
        taps = jnp.concatenate([ext * cw_ref[kk:kk + 1, cols].astype(BF16) for kk in taps_off]
                               + [cur * cw_ref[CONV_HALO:CONV_HALO + 1, cols].astype(BF16)], axis=0)
        acc = _dot(shift_ref[0], taps) + cb_ref[:, cols]
        xconv_ref[0, :, cols] = (acc * jax.nn.sigmoid(acc)).astype(BF16)
    prev_scr[...] = xc_ref[0, CHUNK - BF16_SUBLANES:, :]

    total = cum[CHUNK - 1:CHUNK, :]
    first = cum[0:1, :]
    lane64 = lax.broadcasted_iota(jnp.int32, dt.shape, 1)
    fwd_lane = lane64 < SSM_HEADS

    pad = jnp.zeros((CHUNK, LANES - 2 * SSM_HEADS), F32)
    wide = lambda t: jnp.concatenate([t, pad[:t.shape[0]]], axis=1)
    cum2 = cum * LOG2E
    tcol_ref[0, 0, 0:CHUNK] = wide(cum2)
    tcol_ref[0, 0, CHUNK:2 * CHUNK] = wide(jnp.exp(cum))
    tcol_ref[0, 0, 2 * CHUNK:3 * CHUNK] = wide(jnp.exp(jnp.where(fwd_lane, 0.0, first - cum)) * dt)
    tcol_ref[0, 0, 3 * CHUNK:] = wide(jnp.broadcast_to(jnp.exp(first), (8, 2 * SSM_HEADS)))
    row_t = wide(cum2 - jnp.log2(dt)).T
    dt_t = wide(dt).T
    diag_t = jnp.log2(dt_t[0:SSM_HEADS] + dt_t[SSM_HEADS:2 * SSM_HEADS])
    full = jnp.concatenate([row_t[0:2 * SSM_HEADS], diag_t,
                            jnp.zeros((LANES - 3 * SSM_HEADS, CHUNK), F32)], axis=0)
    swapped = pltpu.roll(full, CHUNK // 2, axis=1)
    next_row = lambda t: pltpu.roll(t, LANES - 1, axis=0)
    low = lax.broadcasted_iota(jnp.int32, full.shape, 1) < CHUNK // 2
    trow_ref[0, 0, 0] = jnp.where(low, full, next_row(swapped))[0:TROW_ROWS]
    trow_ref[0, 0, 1] = jnp.where(low, swapped, next_row(full))[0:TROW_ROWS]

    w = jnp.exp(jnp.where(fwd_lane, total - cum, 0.0)) * dt
    w_wide = wide(w)
    cdec_wide = wide(jnp.broadcast_to(jnp.exp(total), (8, 2 * SSM_HEADS)))
    tiles_per_group = GROUP_WIDTH // LANES
    for g in range(SSM_GROUPS):
        lo, hi = g * GROUP_WIDTH, (g + 1) * GROUP_WIDTH
        tiles = range(g * tiles_per_group, (g + 1) * tiles_per_group)
        wexp = jnp.concatenate([_head_tile(w_wide, 0, t) for t in tiles], axis=1)
        cdec = jnp.concatenate([_head_tile(cdec_wide, 0, t)[0:1, :] for t in tiles], axis=1)
        xw = (xconv_ref[0, :, lo:hi].astype(F32) * wexp).astype(BF16)
        bg = xconv_ref[0, :, SSM_INNER + g * SSM_STATE:SSM_INNER + (g + 1) * SSM_STATE]
        st = state_scr[g]
        sin_ref[0, 0, g] = st.astype(BF16)
        state_scr[g] = st * cdec + _dot_tn(bg, xw)


def _ssd_main_kernel(nc, xc_ref, tcol_ref, trow_ref, z_ref, sin_ref, attn_ref, gate_ref, xres_ref,
                     half_ref, e2f_ref, dskip_ref, snorm_ref,
                     wbr_ref, wout_ref, out_ref, y_scr, state_scr, exp_scr, yn_scr, cb_scr, m_scr):
    c = pl.program_id(1)

    @pl.when(c == 0)
    def _():
        state_scr[...] = jnp.zeros_like(state_scr)
        yn_scr[...] = jnp.zeros_like(yn_scr)

    def project():
        y_ssm = _dot(yn_scr[...], wbr_ref[...])
        merged = attn_ref[0].astype(F32) + gate_ref[0].astype(F32) * y_ssm
        out_ref[0] = xres_ref[0] + _dot(merged.astype(BF16), wout_ref[...])

    pl.when(c == nc)(project)

    @pl.when(c < nc)
    def _():
        _ssd_chunk(project, xc_ref, tcol_ref, trow_ref, z_ref, sin_ref, half_ref, e2f_ref, dskip_ref,
                   snorm_ref, y_scr, state_scr, exp_scr, yn_scr, cb_scr, m_scr)


def _ssd_chunk(project, xc_ref, tcol_ref, trow_ref, z_ref, sin_ref, half_ref, e2f_ref, dskip_ref,
               snorm_ref, y_scr, state_scr, exp_scr, yn_scr, cb_scr, m_scr):
    half = CHUNK // 2
    low = lax.broadcasted_iota(jnp.int32, (CHUNK, LANES), 1) < half

    def b_c(g):
        return (xc_ref[0, :, SSM_INNER + g * SSM_STATE:SSM_INNER + (g + 1) * SSM_STATE],
                xc_ref[0, :, SSM_INNER + SSM_GN + g * SSM_STATE:SSM_INNER + SSM_GN + (g + 1) * SSM_STATE])

    for g in range(SSM_GROUPS):
        bg, cg = b_c(g)
        cb = _dot_nt(cg, bg)
        cb_sw = pltpu.roll(cb, half, axis=1)
        cb_scr[g, 0] = jnp.where(low, cb, cb_sw)
        cb_scr[g, 1] = jnp.where(low, cb_sw, cb)

    project()
    exp_scr[0] = _expand(tcol_ref[0, 0, CHUNK:2 * CHUNK], e2f_ref)

    head_tile = _head_tile

    def expand_terms():
        ecum = tcol_ref[0, 0, CHUNK:2 * CHUNK]
        wb = tcol_ref[0, 0, 2 * CHUNK:3 * CHUNK]
        cdec8 = tcol_ref[0, 0, 3 * CHUNK:]
        cdec_tiles = []
        for t in range(SSM_INNER // LANES):
            cols = slice(t * LANES, (t + 1) * LANES)
            exp_scr[1, :, cols] = head_tile(ecum, SSM_HEADS, t)
            exp_scr[2, :, cols] = head_tile(wb, SSM_HEADS, t)
            cdec_tiles.append(head_tile(cdec8, SSM_HEADS, t)[0:1, :])
        return jnp.concatenate(cdec_tiles, axis=1)

    cum2 = tcol_ref[0, 0, 0:CHUNK]
    src = lax.broadcasted_iota(jnp.int32, (half, LANES), 1) % half
    row = lax.broadcasted_iota(jnp.int32, (half, LANES), 0)
    lower, upper = row > src, row < src
    top, bot = slice(0, half), slice(half, CHUNK)

    def decay_matrices(g):
        for pr in range(SSM_REP // 2):
            h = g * SSM_REP + 2 * pr
            hb = SSM_HEADS + h
            col_f = jnp.take_along_axis(cum2, jnp.where(low, h, h + 1), axis=1)
            col_b = jnp.take_along_axis(cum2, jnp.where(low, hb, hb + 1), axis=1)
            row_f = [trow_ref[0, 0, j, h:h + 1, :] for j in range(2)]
            row_b = [trow_ref[0, 0, j, hb:hb + 1, :] for j in range(2)]
            diag = [trow_ref[0, 0, j, 2 * SSM_HEADS + h:2 * SSM_HEADS + h + 1, :] for j in range(2)]
            mixed = lambda rows, j: jnp.where(lower, col_f[rows] - row_f[j],
                                              jnp.where(upper, col_b[rows] - row_b[j], diag[j]))
            args = [[mixed(top, 0), col_b[top] - row_b[1]],
                    [col_f[bot] - row_f[0], mixed(bot, 1)]]
            for j in range(2):
                for i, rows in enumerate((top, bot)):
                    m_scr[h // 2, rows, j * LANES:(j + 1) * LANES] = (
                        cb_scr[g, j, rows, :] * jnp.exp2(args[i][j])).astype(BF16)

    def inter(g):
        lo, hi = g * GROUP_WIDTH, (g + 1) * GROUP_WIDTH
        bg, cg = b_c(g)
        st = state_scr[g]
        xg = xc_ref[0, :, lo:hi].astype(F32)
        y_scr[:, lo:hi] = (_dot(cg, sin_ref[0, 0, g]) * exp_scr[0, :, lo:hi]
                           + _dot(cg, st.astype(BF16)) * exp_scr[1, :, lo:hi] + dskip_ref[:, lo:hi] * xg)
        xw = (xg * exp_scr[2, :, lo:hi]).astype(BF16)
        state_scr[g] = st * cdec[:, lo:hi] + _dot_tn(bg, xw)

    def intra(g):
        m0, m1 = half_ref[0:1, :], half_ref[1:2, :]
        for pr in range(SSM_REP // 2):
            h = g * SSM_REP + 2 * pr
            cols = slice(h * SSM_HEAD_DIM, h * SSM_HEAD_DIM + LANES)
            xa = xc_ref[0, 0:half, cols]
            xb = xc_ref[0, half:, cols]
            xdiag = jnp.concatenate([xa * m0, xa * m1, xb * m0, xb * m1], axis=0)
            y_scr[:, cols] = (y_scr[:, cols] + _dot(m_scr[h // 2], xdiag)) * z_ref[0, :, cols].astype(F32)

    for g in range(SSM_GROUPS):
        decay_matrices(g)
    cdec = expand_terms()
    for g in range(SSM_GROUPS):
        inter(g)
    for g in range(SSM_GROUPS):
        intra(g)

    yn_scr[...] = _rms(y_scr[...], snorm_ref[...]).astype(BF16)


def _ssd_constants():
    idx = np.arange(CHUNK)
    tril = (idx[:, None] >= idx[None, :]).astype(np.float32)
    heads = np.arange(2 * SSM_HEADS)
    chan_head = np.arange(SSM_INNER) // SSM_HEAD_DIM
    e_f = (heads[:, None] == chan_head[None, :]).astype(np.float32)
    stack = lambda e: jnp.asarray(np.concatenate([e, e], axis=0), BF16)
    widen = lambda e: np.concatenate([e, np.zeros((LANES - e.shape[0], e.shape[1]), np.float32)], axis=0)
    ext_rows = CHUNK + 2 * BF16_SUBLANES
    taps_off = [kk for kk in range(CONV_WIDTH) if kk != CONV_HALO]
    shift = np.zeros((2, CHUNK, len(taps_off) * ext_rows + CHUNK), np.float32)
    shift[:, idx, len(taps_off) * ext_rows + idx] = 1.0
    for n, kk in enumerate(taps_off):
        shift[:, idx, n * ext_rows + BF16_SUBLANES + idx + kk - CONV_HALO] = 1.0
        shift[0, :, n * ext_rows + BF16_SUBLANES + CHUNK:(n + 1) * ext_rows] = 0.0
    lane_half = np.arange(LANES) // SSM_HEAD_DIM
    return dict(tril=jnp.asarray(tril, BF16), triu=jnp.asarray(tril.T, BF16),
                e2f_wide=stack(widen(e_f)), shift=jnp.asarray(shift, BF16),
                half_masks=jnp.asarray(np.stack([lane_half == 0, lane_half == 1]), BF16))


TCOL_ROWS = 3 * CHUNK + 8
TROW_ROWS = 3 * SSM_HEADS


def _ssd_fwd(xbc, dt_raw, conv_w, conv_b, dt_bias, a_neg, consts, b, seq_len):
    tril, triu, shift = consts["tril"], consts["triu"], consts["shift"]
    nc = seq_len // CHUNK
    halo_blocks = CHUNK // BF16_SUBLANES
    n_halo = seq_len // BF16_SUBLANES
    shift_spec = pl.BlockSpec((1,) + shift.shape[1:], lambda bi, c: ((c < nc - 1).astype(jnp.int32), 0, 0))
    return pl.pallas_call(
        _ssd_fwd_kernel,
        grid=(b, nc),
        in_specs=[pl.BlockSpec((1, CHUNK, CONV_DIM), lambda bi, c: (bi, c, 0)),
                  pl.BlockSpec((1, BF16_SUBLANES, CONV_DIM),
                               lambda bi, c: (bi, jnp.minimum((c + 1) * halo_blocks, n_halo - 1), 0)),
                  shift_spec, _const_spec(conv_w.shape), _const_spec(conv_b.shape),
                  pl.BlockSpec((1, CHUNK, 2 * SSM_HEADS), lambda bi, c: (bi, c, 0)),
                  _const_spec(dt_bias.shape), _const_spec(a_neg.shape),
                  _const_spec(tril.shape), _const_spec(triu.shape)],
        out_specs=[pl.BlockSpec((1, CHUNK, CONV_DIM), lambda bi, c: (bi, c, 0)),
                   pl.BlockSpec((1, 1, SSM_GROUPS, SSM_STATE, GROUP_WIDTH), lambda bi, c: (bi, c, 0, 0, 0)),
                   pl.BlockSpec((1, 1, TCOL_ROWS, LANES), lambda bi, c: (bi, c, 0, 0)),
                   pl.BlockSpec((1, 1, 2, TROW_ROWS, CHUNK), lambda bi, c: (bi, c, 0, 0, 0))],
        out_shape=[jax.ShapeDtypeStruct((b, seq_len, CONV_DIM), BF16),
                   jax.ShapeDtypeStruct((b, nc, SSM_GROUPS, SSM_STATE, GROUP_WIDTH), BF16),
                   jax.ShapeDtypeStruct((b, nc, TCOL_ROWS, LANES), F32),
                   jax.ShapeDtypeStruct((b, nc, 2, TROW_ROWS, CHUNK), F32)],
        scratch_shapes=[pltpu.VMEM((SSM_GROUPS, SSM_STATE, GROUP_WIDTH), F32),
                        pltpu.VMEM((BF16_SUBLANES, CONV_DIM), BF16)],
        compiler_params=_params(2),
        name="ssd_fwd",
    )(xbc, xbc, shift, conv_w, conv_b, dt_raw, dt_bias, a_neg, tril, triu)


def _ssd_main(xconv, tcol, trow, z, s_in, attn_part, gate, x3d, consts, d_skip_exp, ssm_norm, w_br_ssm, w_out,
              b, seq_len):
    nc = seq_len // CHUNK
    ssd_chunk = lambda c: nc - 1 - jnp.minimum(c, nc - 1)
    proj_chunk = lambda c: nc - 1 - jnp.maximum(c - 1, 0)
    rev = lambda width: pl.BlockSpec((1, CHUNK, width), lambda bi, c: (bi, ssd_chunk(c), 0))
    rev_chunk = lambda shape: pl.BlockSpec((1, 1) + shape, lambda bi, c: (bi, ssd_chunk(c)) + (0,) * len(shape))
    late = lambda col=0: pl.BlockSpec((1, CHUNK, D_MODEL), lambda bi, c: (bi, proj_chunk(c), col))
    small = (consts["half_masks"], consts["e2f_wide"], d_skip_exp, ssm_norm, w_br_ssm, w_out)
    return pl.pallas_call(
        functools.partial(_ssd_main_kernel, nc),
        grid=(b, nc + 1),
        in_specs=[rev(CONV_DIM), rev_chunk((TCOL_ROWS, LANES)), rev_chunk((2, TROW_ROWS, CHUNK)), rev(SSM_INNER),
                  rev_chunk((SSM_GROUPS, SSM_STATE, GROUP_WIDTH)), late(), late(1), late()]
        + [_const_spec(a.shape) for a in small],
        out_specs=late(),
        out_shape=jax.ShapeDtypeStruct((b, seq_len, D_MODEL), F32),
        scratch_shapes=[pltpu.VMEM((CHUNK, SSM_INNER), F32),
                        pltpu.VMEM((SSM_GROUPS, SSM_STATE, GROUP_WIDTH), F32),
                        pltpu.VMEM((3, CHUNK, SSM_INNER), F32),
                        pltpu.VMEM((CHUNK, SSM_INNER), BF16),
                        pltpu.VMEM((SSM_GROUPS, 2, CHUNK, LANES), F32),
                        pltpu.VMEM((SSM_HEADS // 2, CHUNK, 2 * LANES), BF16)],
        compiler_params=_params(2),
        name="ssd_main",
    )(xconv, tcol, trow, z, s_in, attn_part, gate, x3d, *small)


def _memkv_kernel(mem_ref, g_ref, wkv_ref, kv_ref):
    mn = _rms(mem_ref[0], g_ref[...]).astype(BF16)
    for c0 in range(0, 2 * D_MODEL, PROJ_COLS):
        kv_ref[0, :, c0:c0 + PROJ_COLS] = _dot(mn, wkv_ref[:, c0:c0 + PROJ_COLS]).astype(BF16)


def _memkv(mem, norm_mem, w_kv):
    b, m, _ = mem.shape
    return pl.pallas_call(
        _memkv_kernel,
        grid=(b,),
        in_specs=[pl.BlockSpec((1, m, D_MODEL), lambda bi: (bi, 0, 0)),
                  _const_spec(norm_mem.shape), _const_spec(w_kv.shape)],
        out_specs=pl.BlockSpec((1, m, 2 * D_MODEL), lambda bi: (bi, 0, 0)),
        out_shape=jax.ShapeDtypeStruct((b, m, 2 * D_MODEL), BF16),
        compiler_params=_params(1),
        name="mem_kv",
    )(mem, norm_mem, w_kv)


def _tail_kernel(x_ref, kv_ref, ncross_ref, wq_ref, wo_ref, nffn_ref, wg_ref, wu_ref, wd_ref, nfin_ref,
                 out_ref, o_scr, h_scr, q_scr, s_scr):
    x = x_ref[0]
    u = _rms(x, ncross_ref[...]).astype(BF16)
    for c0 in range(0, D_MODEL, PROJ_COLS):
        q_scr[:, c0:c0 + PROJ_COLS] = (_dot(u, wq_ref[:, c0:c0 + PROJ_COLS]) * XQ_SCALE).astype(BF16)

    def logits(h):
        lo, hi = h * X_HEAD_DIM, (h + 1) * X_HEAD_DIM
        s_scr[h] = _dot_nt(q_scr[:, lo:hi], kv_ref[0, :, lo:hi])

    def attend(h):
        lo, hi = h * X_HEAD_DIM, (h + 1) * X_HEAD_DIM
        s = s_scr[h]
        p = jnp.exp2(s - jnp.max(s, axis=-1, keepdims=True))
        denom = jnp.sum(p, axis=-1, keepdims=True)
        o = _dot(p.astype(BF16), kv_ref[0, :, D_MODEL + lo:D_MODEL + hi]) / denom
        o_scr[:, lo:hi] = o.astype(BF16)

    logits(0)
    for h in range(X_HEADS):
        if h + 1 < X_HEADS:
            logits(h + 1)
        attend(h)
    x = x + _dot(o_scr[...], wo_ref[...])
    u = _rms(x, nffn_ref[...]).astype(BF16)
    for c0 in range(0, FFN_HIDDEN, PROJ_COLS):
        c1 = min(c0 + PROJ_COLS, FFN_HIDDEN)
        gte = _dot(u, wg_ref[:, c0:c1])
        up = _dot(u, wu_ref[:, c0:c1])
        h_scr[:, c0:c1] = (gte * jax.nn.sigmoid(gte) * up).astype(BF16)
    x = x + _dot(h_scr[...], wd_ref[...])
    out_ref[0] = _rms(x, nfin_ref[...])


def _tail(x3d, kv, norm_cross, w_q, w_o, norm_ffn, w_gate, w_up, w_down, norm_final):
    b, seq_len, _ = x3d.shape
    m = kv.shape[1]
    consts = (norm_cross, w_q, w_o, norm_ffn, w_gate, w_up, w_down, norm_final)
    return pl.pallas_call(
        _tail_kernel,
        grid=(b, seq_len // ROW_TILE),
        in_specs=[pl.BlockSpec((1, ROW_TILE, D_MODEL), lambda bi, i: (bi, i, 0)),
                  pl.BlockSpec((1, m, 2 * D_MODEL), lambda bi, i: (bi, 0, 0))]
        + [_const_spec(a.shape) for a in consts],
        out_specs=pl.BlockSpec((1, ROW_TILE, D_MODEL), lambda bi, i: (bi, i, 0)),
        out_shape=jax.ShapeDtypeStruct((b, seq_len, D_MODEL), F32),
        scratch_shapes=[pltpu.VMEM((ROW_TILE, D_MODEL), BF16), pltpu.VMEM((ROW_TILE, FFN_HIDDEN), BF16),
                        pltpu.VMEM((ROW_TILE, D_MODEL), BF16), pltpu.VMEM((X_HEADS, ROW_TILE, m), F32)],
        compiler_params=_params(2),
        name="cross_ffn",
    )(x3d, kv, *consts)


def _prepare(norm_mix, w_in, conv_w, conv_b, attn_sink, a_log, dt_bias, d_skip, ssm_norm, w_br_attn,
             w_br_ssm, w_out, norm_cross, norm_mem, w_q_cross, w_kv_cross, w_o_cross, norm_ffn, w_gate_up,
             w_down, norm_final):
    row = lambda v: v.reshape(1, -1).astype(F32)
    w = w_in[0]
    cuts = np.cumsum([0, ATT_WIDTH, 2 * KV_WIDTH, SSM_INNER, CONV_DIM, 2 * SSM_HEADS, 2 * D_MODEL])
    w = w.astype(BF16)
    wq, wkv, wz, wxbc, wdt, wgate = (w[:, int(s):int(e)] for s, e in zip(cuts[:-1], cuts[1:]))
    grid5 = (ATT_KV_HEADS // 2, 2, ATT_REP)
    reorder = lambda t, axis: jnp.swapaxes(
        t.reshape(t.shape[:axis] + grid5 + (HEAD_DIM,) + t.shape[axis + 1:]), axis + 1, axis + 2).reshape(t.shape)
    order = np.asarray(_ATT_HEAD_ORDER)
    slopes = jnp.exp2(-8.0 * jnp.arange(1, ATT_HEADS + 1, dtype=F32) / ATT_HEADS)
    return dict(
        norm_mix=row(norm_mix[0]), wq=reorder(wq, 1), wkv=wkv, wz=wz, wxbc=wxbc, wdt=wdt, wgate=wgate,
        conv_w=conv_w[0].astype(F32), conv_b=row(conv_b[0]),
        slopes=slopes[order] * LOG2E, sink=attn_sink[0].astype(F32)[order] * LOG2E,
        a_neg=row(-jnp.exp(a_log[0].astype(F32))), dt_bias=row(dt_bias[0]),
        d_skip=row(jnp.repeat(d_skip[0].astype(F32), SSM_HEAD_DIM)), ssm_norm=row(ssm_norm[0]),
        w_br_attn=reorder(w_br_attn[0].astype(BF16), 0), w_br_ssm=w_br_ssm[0].astype(BF16),
        w_out=w_out[0].astype(BF16),
        norm_cross=row(norm_cross[0]), norm_mem=row(norm_mem[0]),
        w_q=w_q_cross[0].astype(BF16), w_kv=w_kv_cross[0].astype(BF16), w_o=w_o_cross[0].astype(BF16),
        norm_ffn=row(norm_ffn[0]), w_gate=w_gate_up[0][:, :FFN_HIDDEN].astype(BF16),
        w_up=w_gate_up[0][:, FFN_HIDDEN:].astype(BF16), w_down=w_down[0].astype(BF16),
        norm_final=row(norm_final), consts=_ssd_constants())


def _trunk(x, mem, p):
    b, seq_len, _ = x.shape
    assert seq_len % ROW_TILE == 0 and seq_len % ATT_TQ == 0 and seq_len % CHUNK == 0
    qkv, z, xbc, dt_raw, gate = _inproj(x.reshape(b * seq_len, D_MODEL), p["norm_mix"], p["wq"], p["wkv"],
                                        p["wz"], p["wxbc"], p["wdt"], p["wgate"])
    as3d = lambda t: t.reshape(b, seq_len, t.shape[-1])
    qkv, z, xbc, dt_raw, gate = as3d(qkv), as3d(z), as3d(xbc), as3d(dt_raw), as3d(gate)
    attn_part = _attention(qkv, gate, p["slopes"], p["sink"], p["w_br_attn"], b, seq_len)
    xconv, s_in, tcol, trow = _ssd_fwd(xbc, dt_raw, p["conv_w"], p["conv_b"], p["dt_bias"], p["a_neg"],
                                       p["consts"], b, seq_len)
    x1 = _ssd_main(xconv, tcol, trow, z, s_in, attn_part, gate, x, p["consts"], p["d_skip"], p["ssm_norm"],
                   p["w_br_ssm"], p["w_out"], b, seq_len)
    kv = _memkv(mem, p["norm_mem"], p["w_kv"])
    return _tail(x1, kv, p["norm_cross"], p["w_q"], p["w_o"], p["norm_ffn"], p["w_gate"], p["w_up"],
                 p["w_down"], p["norm_final"])


def kernel(x_prompt, x_sample, mem_prompt, mem_sample, norm_mix, w_in, conv_w, conv_b, attn_sink, a_log,
           dt_bias, d_skip, ssm_norm, w_br_attn, w_br_ssm, w_out, norm_cross, norm_mem, w_q_cross,
           w_kv_cross, w_o_cross, norm_ffn, w_gate_up, w_down, norm_final):
    p = _prepare(norm_mix, w_in, conv_w, conv_b, attn_sink, a_log, dt_bias, d_skip, ssm_norm, w_br_attn,
                 w_br_ssm, w_out, norm_cross, norm_mem, w_q_cross, w_kv_cross, w_o_cross, norm_ffn,
                 w_gate_up, w_down, norm_final)
    return (_trunk(x_prompt, mem_prompt, p), _trunk(x_sample, mem_sample, p))
```
